```python
import jax, jax.numpy as jnp
from jax import lax
import numpy as np

D_MODEL = 2048
BATCH = 4
SEQ = 2048
DEPTH = 2
DEC_BATCH = 128
DEC_SEQ = 8
PAST_LEN = 2048
PAGE_SIZE = 128

N_HEADS = 8
HEAD_DIM = 128
D_ATTN = N_HEADS * HEAD_DIM
D_RNN = 1024
D_BRANCH = D_RNN
RNN_BLOCKS = 8
RNN_BLOCK_W = D_RNN // RNN_BLOCKS
RNN_CONV_W = 4
RG_C = 8.0
D_FF = 3 * D_MODEL
FFN_CONV_W = 3
N_BRANCH = 2
Q_BLOCK = 128
NORM_EPS = 1e-6
N_IN = 2 * D_RNN + 3 * D_ATTN + N_HEADS + N_BRANCH * D_MODEL

kernel_name = "hawk_fox_parallel_hybrid_step"


def rmsnorm(x, g):
    xf = x.astype(jnp.float32)
    inv = lax.rsqrt(jnp.mean(xf * xf, axis=-1, keepdims=True) + NORM_EPS)
    return (xf * inv * g.astype(jnp.float32)).astype(x.dtype)


def causal_dwconv(x, prev, w, b):
    width = w.shape[0]
    t = x.shape[1]
    xp = jnp.concatenate([prev.astype(x.dtype), x], axis=1)
    y = b
    for j in range(width):
        y = y + w[j] * xp[:, j:j + t]
    return y, xp[:, t:]


def rglru(x, h0, w_a, b_a, w_x, b_x, lam):
    b, t, c = x.shape
    xf = x.astype(jnp.float32)
    xb = xf.reshape(b, t, RNN_BLOCKS, RNN_BLOCK_W)
    r = jax.nn.sigmoid(jnp.einsum('btnc,ncd->btnd', xb, w_a.astype(jnp.float32)).reshape(b, t, c) + b_a.astype(jnp.float32))
    i = jax.nn.sigmoid(jnp.einsum('btnc,ncd->btnd', xb, w_x.astype(jnp.float32)).reshape(b, t, c) + b_x.astype(jnp.float32))
    log_a = -RG_C * r * jax.nn.softplus(-lam.astype(jnp.float32))
    a = jnp.exp(log_a)
    u = jnp.sqrt(-jnp.expm1(2.0 * log_a)) * (i * xf)
    u = u.at[:, 0].add(a[:, 0] * h0.astype(jnp.float32))

    def combine(c1, c2):
        a1, b1 = c1
        a2, b2 = c2
        return a1 * a2, a2 * b1 + b2

    _, h = lax.associative_scan(combine, (a, u), axis=1)
    return h.astype(x.dtype), h[:, -1].astype(x.dtype)


def fox_block_attention(q, k, v, c, q_offset):
    tq = q.shape[1]
    scale = HEAD_DIM ** -0.5
    outs = []
    for start in range(0, tq, Q_BLOCK):
        end = min(start + Q_BLOCK, tq)
        kend = q_offset + end
        s = jnp.einsum('bqhd,bkhd->bhqk', q[:, start:end], k[:, :kend], preferred_element_type=jnp.float32) * scale
        cq = c[:, q_offset + start:q_offset + end].transpose(0, 2, 1)
        ck = c[:, :kend].transpose(0, 2, 1)
        q_pos = q_offset + start + jnp.arange(end - start)
        k_pos = jnp.arange(kend)
        mask = k_pos[None, :] <= q_pos[:, None]
        s = jnp.where(mask, s + cq[..., :, None] - ck[..., None, :], -1e30)
        p = jax.nn.softmax(s, axis=-1)
        outs.append(jnp.einsum('bhqk,bkhd->bqhd', p.astype(v.dtype), v[:, :kend]))
    return jnp.concatenate(outs, axis=1)


def mixer(xn, w_in, conv_w, conv_b, w_a, b_a, w_x, b_x, lam, b_f, w_branch, w_out,
          past_k, past_v, past_logf, h0, conv_prev):
    b, t, _ = xn.shape
    proj = xn @ w_in
    offs = list(np.cumsum([D_RNN, D_RNN, D_ATTN, D_ATTN, D_ATTN, N_HEADS]))
    xr, xg, q, k, v, f_logit, gates = jnp.split(proj, offs, axis=-1)
    xc, conv_new = causal_dwconv(xr, conv_prev, conv_w, conv_b)
    hseq, h_last = rglru(xc, h0, w_a, b_a, w_x, b_x, lam)
    y_a = hseq * jax.nn.gelu(xg)
    q = q.reshape(b, t, N_HEADS, HEAD_DIM)
    k = k.reshape(b, t, N_HEADS, HEAD_DIM)
    v = v.reshape(b, t, N_HEADS, HEAD_DIM)
    logf = jax.nn.log_sigmoid((f_logit + b_f).astype(jnp.float32))
    k_all = jnp.concatenate([past_k.astype(k.dtype), k], axis=1)
    v_all = jnp.concatenate([past_v.astype(v.dtype), v], axis=1)
    c = jnp.cumsum(jnp.concatenate([past_logf.astype(jnp.float32), logf], axis=1), axis=1)
    y_b = fox_block_attention(q, k_all, v_all, c, past_k.shape[1]).reshape(b, t, D_ATTN)
    branches = jnp.stack([y_a, y_b], axis=2)
    pb = jnp.einsum('btnw,nwd->btnd', branches, w_branch)
    g = jax.nn.sigmoid(gates.reshape(b, t, N_BRANCH, D_MODEL))
    y = jnp.sum(g * pb, axis=2) @ w_out
    return y, (k, v, logf.astype(xn.dtype), h_last, conv_new)


def conv_ffn(xn, w_up, conv_w, conv_b, w_down, conv_prev):
    up = xn @ w_up
    upc, conv_new = causal_dwconv(up, conv_prev, conv_w, conv_b)
    gate, val = jnp.split(upc, 2, axis=-1)
    return (jax.nn.gelu(gate) * val) @ w_down, conv_new


def setup_inputs(seed: int = 0) -> dict:
    key = jax.random.key(seed)
    ks = jax.random.split(key, 32)
    n_pages = PAST_LEN // PAGE_SIZE
    n_used = DEC_BATCH * n_pages
    n_pool = n_used + n_used // 4

    def nrm(k, shape, scale):
        return scale * jax.random.normal(k, shape, jnp.float32)

    a0 = jax.random.uniform(ks[17], (DEPTH, D_RNN), jnp.float32, 0.9, 0.999)
    return {
        'x_prompt': nrm(ks[0], (BATCH, SEQ, D_MODEL), 1.0),
        'x_sample': nrm(ks[1], (DEC_BATCH, DEC_SEQ, D_MODEL), 1.0),
        'cache_k': nrm(ks[2], (DEPTH, n_pool, PAGE_SIZE, N_HEADS, HEAD_DIM), 1.0),
        'cache_v': nrm(ks[3], (DEPTH, n_pool, PAGE_SIZE, N_HEADS, HEAD_DIM), 1.0),
        'cache_logf': jax.nn.log_sigmoid(2.0 + nrm(ks[4], (DEPTH, n_pool, PAGE_SIZE, N_HEADS), 1.0)),
        'page_table': jax.random.permutation(ks[5], n_pool)[:n_used].reshape(DEC_BATCH, n_pages).astype(jnp.int32),
        'state_rnn_h': nrm(ks[6], (DEPTH, DEC_BATCH, D_RNN), 0.5),
        'state_rnn_conv': nrm(ks[7], (DEPTH, DEC_BATCH, RNN_CONV_W - 1, D_RNN), 1.0),
        'state_ffn_conv': nrm(ks[8], (DEPTH, DEC_BATCH, FFN_CONV_W - 1, 2 * D_FF), 1.0),
        'norm_mix_g': 1.0 + nrm(ks[9], (DEPTH, D_MODEL), 0.02),
        'w_in': nrm(ks[10], (DEPTH, D_MODEL, N_IN), D_MODEL ** -0.5),
        'rnn_conv_w': nrm(ks[11], (DEPTH, RNN_CONV_W, D_RNN), RNN_CONV_W ** -0.5),
        'rnn_conv_b': nrm(ks[12], (DEPTH, D_RNN), 0.01),
        'rg_w_a': nrm(ks[13], (DEPTH, RNN_BLOCKS, RNN_BLOCK_W, RNN_BLOCK_W), RNN_BLOCK_W ** -0.5),
        'rg_b_a': nrm(ks[14], (DEPTH, D_RNN), 0.01),
        'rg_w_x': nrm(ks[15], (DEPTH, RNN_BLOCKS, RNN_BLOCK_W, RNN_BLOCK_W), RNN_BLOCK_W ** -0.5),
        'rg_b_x': nrm(ks[16], (DEPTH, D_RNN), 0.01),
        'rg_lambda': jnp.log(a0) - jnp.log1p(-a0),
        'fox_b_f': 2.0 + nrm(ks[18], (DEPTH, N_HEADS), 0.1),
        'w_branch': nrm(ks[19], (DEPTH, N_BRANCH, D_BRANCH, D_MODEL), D_BRANCH ** -0.5),
        'w_out': nrm(ks[20], (DEPTH, D_MODEL, D_MODEL), D_MODEL ** -0.5),
        'norm_ffn_g': 1.0 + nrm(ks[21], (DEPTH, D_MODEL), 0.02),
        'w_up': nrm(ks[22], (DEPTH, D_MODEL, 2 * D_FF), D_MODEL ** -0.5),
        'ffn_conv_w': nrm(ks[23], (DEPTH, FFN_CONV_W, 2 * D_FF), FFN_CONV_W ** -0.5),
        'ffn_conv_b': nrm(ks[24], (DEPTH, 2 * D_FF), 0.01),
        'w_down': nrm(ks[25], (DEPTH, D_FF, D_MODEL), D_FF ** -0.5),
        'norm_final_g': 1.0 + nrm(ks[26], (D_MODEL,), 0.02),
    }


def reference(x_prompt, x_sample, cache_k, cache_v, cache_logf, page_table, state_rnn_h, state_rnn_conv,
              state_ffn_conv, norm_mix_g, w_in, rnn_conv_w, rnn_conv_b, rg_w_a, rg_b_a, rg_w_x, rg_b_x,
              rg_lambda, fox_b_f, w_branch, w_out, norm_ffn_g, w_up, ffn_conv_w, ffn_conv_b, w_down,
              norm_final_g):
    n_pages = page_table.shape[1]
    past_len = n_pages * PAGE_SIZE
    b_p = x_prompt.shape[0]
    b_s = x_sample.shape[0]
    xp, xs = x_prompt, x_sample
    kp, vp, lfp, hp, rcp, fcp = [], [], [], [], [], []
    ksm, vsm, lfs, hs, rcs, fcs = [], [], [], [], [], []
    for l in range(DEPTH):
        mix_w = (w_in[l], rnn_conv_w[l], rnn_conv_b[l], rg_w_a[l], rg_b_a[l], rg_w_x[l], rg_b_x[l],
                 rg_lambda[l], fox_b_f[l], w_branch[l], w_out[l])
        ffn_w = (w_up[l], ffn_conv_w[l], ffn_conv_b[l], w_down[l])
        empty_kv = jnp.zeros((b_p, 0, N_HEADS, HEAD_DIM), xp.dtype)
        empty_lf = jnp.zeros((b_p, 0, N_HEADS), jnp.float32)
        y, (k_n, v_n, lf_n, h_n, rc_n) = mixer(
            rmsnorm(xp, norm_mix_g[l]), *mix_w, empty_kv, empty_kv, empty_lf,
            jnp.zeros((b_p, D_RNN), xp.dtype), jnp.zeros((b_p, RNN_CONV_W - 1, D_RNN), xp.dtype))
        xp = xp + y
        y, fc_n = conv_ffn(rmsnorm(xp, norm_ffn_g[l]), *ffn_w,
                           jnp.zeros((b_p, FFN_CONV_W - 1, 2 * D_FF), xp.dtype))
        xp = xp + y
        kp.append(k_n); vp.append(v_n); lfp.append(lf_n); hp.append(h_n); rcp.append(rc_n); fcp.append(fc_n)
        k_past = cache_k[l, page_table].reshape(b_s, past_len, N_HEADS, HEAD_DIM)
        v_past = cache_v[l, page_table].reshape(b_s, past_len, N_HEADS, HEAD_DIM)
        lf_past = cache_logf[l, page_table].reshape(b_s, past_len, N_HEADS)
        y, (k_n, v_n, lf_n, h_n, rc_n) = mixer(
            rmsnorm(xs, norm_mix_g[l]), *mix_w, k_past, v_past, lf_past, state_rnn_h[l], state_rnn_conv[l])
        xs = xs + y
        y, fc_n = conv_ffn(rmsnorm(xs, norm_ffn_g[l]), *ffn_w, state_ffn_conv[l])
        xs = xs + y
        ksm.append(k_n); vsm.append(v_n); lfs.append(lf_n); hs.append(h_n); rcs.append(rc_n); fcs.append(fc_n)
    y_prompt = rmsnorm(xp, norm_final_g)
    y_sample = rmsnorm(xs, norm_final_g)
    return (y_prompt, y_sample,
            jnp.stack(kp), jnp.stack(vp), jnp.stack(lfp), jnp.stack(hp), jnp.stack(rcp), jnp.stack(fcp),
            jnp.stack(ksm), jnp.stack(vsm), jnp.stack(lfs), jnp.stack(hs), jnp.stack(rcs), jnp.stack(fcs))
```

```python
import functools

import jax
import jax.numpy as jnp
from jax import lax
from jax.experimental import pallas as pl
from jax.experimental.pallas import tpu as pltpu

HEAD_DIM = 128
RNN_BLOCK_W = 128
RG_C = 8.0
NORM_EPS = 1e-6
SUBLANES = 8
LANES = 128
MASK_VALUE = -1e30
VMEM_LIMIT_BYTES = 56 * 1024 * 1024


def _cparams(semantics):
    return pltpu.CompilerParams(dimension_semantics=semantics, vmem_limit_bytes=VMEM_LIMIT_BYTES)


def _tile(dim, pref):
    t = min(dim, pref)
    assert dim % t == 0, (dim, pref)
    return t


def _nt_dot(a, b):
    return lax.dot_general(a, b, (((1,), (1,)), ((), ())), preferred_element_type=jnp.float32)


def _rmsnorm_rows(x, g):
    inv = lax.rsqrt(jnp.mean(x * x, axis=-1, keepdims=True) + NORM_EPS)
    return x * inv * g


def _rmsnorm_kernel(x_ref, g_ref, o_ref):
    o_ref[...] = _rmsnorm_rows(x_ref[...], g_ref[...]).astype(o_ref.dtype)


def rmsnorm(x, g, out_dtype):
    m, d = x.shape
    tm = _tile(m, 512)
    return pl.pallas_call(
        _rmsnorm_kernel,
        grid=(m // tm,),
        in_specs=[pl.BlockSpec((tm, d), lambda i: (i, 0)),
                  pl.BlockSpec((1, d), lambda i: (0, 0))],
        out_specs=pl.BlockSpec((tm, d), lambda i: (i, 0)),
        out_shape=jax.ShapeDtypeStruct((m, d), out_dtype),
        compiler_params=_cparams(("arbitrary",)),
        name="rmsnorm",
    )(x, g.reshape(1, d))


def _inproj_kernel(x_ref, g_ref, w_ref, wf_ref, bf_ref, proj_ref, logf_ref, xn_ref):
    @pl.when(pl.program_id(1) == 0)
    def _():
        xn = _rmsnorm_rows(x_ref[...], g_ref[...]).astype(jnp.bfloat16)
        xn_ref[...] = xn
        f = jnp.dot(xn, wf_ref[...], preferred_element_type=jnp.float32)
        logf_ref[...] = jax.nn.log_sigmoid(f + bf_ref[...])

    proj_ref[...] = jnp.dot(xn_ref[...], w_ref[...], preferred_element_type=jnp.float32)


def in_projection(x, g, w_main, w_f, b_f):
    m, d = x.shape
    n = w_main.shape[1]
    tm = _tile(m, 1024)
    tn = _tile(n, 1024)
    return pl.pallas_call(
        _inproj_kernel,
        grid=(m // tm, n // tn),
        in_specs=[pl.BlockSpec((tm, d), lambda i, j: (i, 0)),
                  pl.BlockSpec((1, d), lambda i, j: (0, 0)),
                  pl.BlockSpec((d, tn), lambda i, j: (0, j)),
                  pl.BlockSpec((d, LANES), lambda i, j: (0, 0)),
                  pl.BlockSpec((1, LANES), lambda i, j: (0, 0))],
        out_specs=[pl.BlockSpec((tm, tn), lambda i, j: (i, j)),
                   pl.BlockSpec((tm, LANES), lambda i, j: (i, 0))],
        out_shape=[jax.ShapeDtypeStruct((m, n), jnp.float32),
                   jax.ShapeDtypeStruct((m, LANES), jnp.float32)],
        scratch_shapes=[pltpu.VMEM((tm, d), jnp.bfloat16)],
        compiler_params=_cparams(("arbitrary", "arbitrary")),
        name="in_projection",
    )(x, g.reshape(1, d), w_main, w_f, b_f)


def _causal_dwconv(x, prev, pos, cw_ref, cb_ref, width):
    rows = x.shape[0]
    y = cb_ref[...] + cw_ref[width - 1:width, :] * x
    for d in range(1, width):
        up = width - 1 - d
        before = prev if up == 0 else pltpu.roll(prev, rows - up, axis=0)
        shifted = jnp.where(pos >= d, pltpu.roll(x, d, axis=0), before)
        y = y + cw_ref[width - 1 - d:width - d, :] * shifted
    return y


def _tail_rows(x, width):
    rows = x.shape[0]
    return pltpu.roll(x[rows - SUBLANES:, :], width - 1, axis=0)


def _rglru_core(x, xg, prev, pos, period, h_in, cw_ref, cb_ref, wa_ref, wx_ref, ba_ref, bx_ref, lam_ref, width):
    tc = x.shape[1]
    xc = _causal_dwconv(x, prev, pos, cw_ref, cb_ref, width)
    ga, gx = [], []
    for nb in range(tc // RNN_BLOCK_W):
        xb = xc[:, nb * RNN_BLOCK_W:(nb + 1) * RNN_BLOCK_W].astype(jnp.bfloat16)
        ga.append(jnp.dot(xb, wa_ref[nb], preferred_element_type=jnp.float32))
        gx.append(jnp.dot(xb, wx_ref[nb], preferred_element_type=jnp.float32))
    r_gate = jax.nn.sigmoid(jnp.concatenate(ga, axis=1) + ba_ref[...])
    i_gate = jax.nn.sigmoid(jnp.concatenate(gx, axis=1) + bx_ref[...])
    log_a = -RG_C * r_gate * jax.nn.softplus(-lam_ref[...])
    a = jnp.exp(log_a)
    u = jnp.sqrt(-jnp.tanh(log_a) * (a * a + 1.0)) * (i_gate * xc)
    s = 1
    while s < period:
        keep = pos >= s
        a_prev = jnp.where(keep, pltpu.roll(a, s, axis=0), 1.0)
        u_prev = jnp.where(keep, pltpu.roll(u, s, axis=0), 0.0)
        u = a * u_prev + u
        a = a * a_prev
        s *= 2
    h = a * h_in + u
    return h, h * jax.nn.gelu(xg)


def _rglru_prompt_kernel(xr_ref, xg_ref, cw_ref, cb_ref, wa_ref, wx_ref, ba_ref, bx_ref, lam_ref,
                         ya_ref, hl_ref, prev_ref, hc_ref, *, width):
    tt = xr_ref.shape[0]

    @pl.when(pl.program_id(2) == 0)
    def _():
        prev_ref[...] = jnp.zeros_like(prev_ref)
        hc_ref[...] = jnp.zeros_like(hc_ref)

    x = xr_ref[...]
    pos = lax.broadcasted_iota(jnp.int32, x.shape, 0)
    h_in = hc_ref[SUBLANES - 1:SUBLANES, :]
    h, ya = _rglru_core(x, xg_ref[...], prev_ref[...], pos, tt, h_in, cw_ref, cb_ref, wa_ref, wx_ref,
                        ba_ref, bx_ref, lam_ref, width)
    ya_ref[...] = ya.astype(ya_ref.dtype)
    hc_ref[...] = h[tt - SUBLANES:, :]
    hl_ref[...] = h[tt - 1:tt, :]
    prev_ref[0:SUBLANES, :] = _tail_rows(x, width)


def _rglru_sample_kernel(xr_ref, xg_ref, cw_ref, cb_ref, wa_ref, wx_ref, ba_ref, bx_ref, lam_ref,
                         prev_ref, h0_ref, ya_ref, hl_ref, hs_ref, *, width, seq):
    x = xr_ref[...]
    pos = lax.broadcasted_iota(jnp.int32, x.shape, 0) % seq
    h, ya = _rglru_core(x, xg_ref[...], prev_ref[...], pos, seq, h0_ref[...], cw_ref, cb_ref, wa_ref,
                        wx_ref, ba_ref, bx_ref, lam_ref, width)
    ya_ref[...] = ya.astype(ya_ref.dtype)
    for j in range(h.shape[1] // LANES):
        hs_ref[j] = h[:, j * LANES:(j + 1) * LANES]
        hl_ref[:, j * LANES:(j + 1) * LANES] = hs_ref[j, pl.ds(seq - 1, hl_ref.shape[0], stride=seq), :]


def rglru_prompt(proj, batch, seq, d_rnn, conv_w, conv_b, w_a, w_x, b_a, b_x, lam):
    m = proj.shape[0]
    width = conv_w.shape[0]
    tt = _tile(seq, 512)
    tc = _tile(d_rnn, 256)
    nt, nc = seq // tt, d_rnn // tc
    nb = tc // RNN_BLOCK_W
    row = lambda c, b, t: (b * nt + t, c)
    vec = lambda c, b, t: (0, c)
    ya, hl = pl.pallas_call(
        functools.partial(_rglru_prompt_kernel, width=width),
        grid=(nc, batch, nt),
        in_specs=[pl.BlockSpec((tt, tc), row),
                  pl.BlockSpec((tt, tc), lambda c, b, t: (b * nt + t, nc + c)),
                  pl.BlockSpec((width, tc), vec),
                  pl.BlockSpec((1, tc), vec),
                  pl.BlockSpec((nb, RNN_BLOCK_W, RNN_BLOCK_W), lambda c, b, t: (c, 0, 0)),
                  pl.BlockSpec((nb, RNN_BLOCK_W, RNN_BLOCK_W), lambda c, b, t: (c, 0, 0)),
                  pl.BlockSpec((1, tc), vec),
                  pl.BlockSpec((1, tc), vec),
                  pl.BlockSpec((1, tc), vec)],
        out_specs=[pl.BlockSpec((tt, tc), row),
                   pl.BlockSpec((None, 1, tc), lambda c, b, t: (b, 0, c))],
        out_shape=[jax.ShapeDtypeStruct((m, d_rnn), jnp.bfloat16),
                   jax.ShapeDtypeStruct((batch, 1, d_rnn), jnp.float32)],
        scratch_shapes=[pltpu.VMEM((tt, tc), jnp.float32),
                        pltpu.VMEM((SUBLANES, tc), jnp.float32)],
        compiler_params=_cparams(("arbitrary", "arbitrary", "arbitrary")),
        name="rglru_prompt",
    )(proj, proj, conv_w, conv_b, w_a, w_x, b_a, b_x, lam)
    return ya, hl.reshape(batch, d_rnn)


def rglru_sample(proj, batch, seq, d_rnn, conv_w, conv_b, w_a, w_x, b_a, b_x, lam, prev_rows, h0_rows):
    m = proj.shape[0]
    width = conv_w.shape[0]
    tt = _tile(m, 512)
    tc = _tile(d_rnn, 256)
    nc = d_rnn // tc
    nb = tc // RNN_BLOCK_W
    row = lambda c, t: (t, c)
    vec = lambda c, t: (0, c)
    ya, hl = pl.pallas_call(
        functools.partial(_rglru_sample_kernel, width=width, seq=seq),
        grid=(nc, m // tt),
        in_specs=[pl.BlockSpec((tt, tc), row),
                  pl.BlockSpec((tt, tc), lambda c, t: (t, nc + c)),
                  pl.BlockSpec((width, tc), vec),
                  pl.BlockSpec((1, tc), vec),
                  pl.BlockSpec((nb, RNN_BLOCK_W, RNN_BLOCK_W), lambda c, t: (c, 0, 0)),
                  pl.BlockSpec((nb, RNN_BLOCK_W, RNN_BLOCK_W), lambda c, t: (c, 0, 0)),
                  pl.BlockSpec((1, tc), vec),
                  pl.BlockSpec((1, tc), vec),
                  pl.BlockSpec((1, tc), vec),
                  pl.BlockSpec((tt, tc), row),
                  pl.BlockSpec((tt, tc), row)],
        out_specs=[pl.BlockSpec((tt, tc), row),
                   pl.BlockSpec((tt // seq, tc), row)],
        out_shape=[jax.ShapeDtypeStruct((m, d_rnn), jnp.bfloat16),
                   jax.ShapeDtypeStruct((batch, d_rnn), jnp.float32)],
        scratch_shapes=[pltpu.VMEM((tc // LANES, tt, LANES), jnp.float32)],
        compiler_params=_cparams(("arbitrary", "arbitrary")),
        name="rglru_sample",
    )(proj, proj, conv_w, conv_b, w_a, w_x, b_a, b_x, lam, prev_rows, h0_rows)
    return ya, hl


def _lane_cumsum(x):
    lane = lax.broadcasted_iota(jnp.int32, x.shape, 1)
    s = 1
    while s < x.shape[1]:
        x = x + jnp.where(lane >= s, pltpu.roll(x, s, axis=1), 0.0)
        s *= 2
    return x


def _cumsum_prompt_kernel(lf_ref, c_ref):
    c_ref[...] = _lane_cumsum(lf_ref[...])


def _cumsum_sample_kernel(pt_ref, *refs):
    del pt_ref
    c_ref = refs[-1]
    c_ref[...] = _lane_cumsum(jnp.concatenate([r[...] for r in refs[:-1]], axis=1))


def cumsum_prompt(lf_rows):
    b, h, t = lf_rows.shape
    return pl.pallas_call(
        _cumsum_prompt_kernel,
        grid=(b,),
        in_specs=[pl.BlockSpec((None, h, t), lambda i: (i, 0, 0))],
        out_specs=pl.BlockSpec((None, h, t), lambda i: (i, 0, 0)),
        out_shape=jax.ShapeDtypeStruct((b, h, t), jnp.float32),
        compiler_params=_cparams(("arbitrary",)),
        name="cumsum_prompt",
    )(lf_rows)


def cumsum_sample(page_table_flat, n_pages, lf_pool_rows, lf_new_rows):
    b, h, tn = lf_new_rows.shape
    page = lf_pool_rows.shape[2]
    total = n_pages * page + tn

    def page_spec(i):
        return pl.BlockSpec((None, h, page), lambda s, pt: (pt[s * n_pages + i], 0, 0))

    return pl.pallas_call(
        _cumsum_sample_kernel,
        grid_spec=pltpu.PrefetchScalarGridSpec(
            num_scalar_prefetch=1,
            grid=(b,),
            in_specs=[page_spec(i) for i in range(n_pages)]
            + [pl.BlockSpec((None, h, tn), lambda s, pt: (s, 0, 0))],
            out_specs=pl.BlockSpec((None, h, total), lambda s, pt: (s, 0, 0)),
        ),
        out_shape=jax.ShapeDtypeStruct((b, h, total), jnp.float32),
        compiler_params=_cparams(("arbitrary",)),
        name="cumsum_sample",
    )(page_table_flat, *([lf_pool_rows] * n_pages), lf_new_rows)


def _softmax_update(s, v, m_ref, l_ref, acc_ref, h):
    hs = slice(h * HEAD_DIM, (h + 1) * HEAD_DIM)
    m_prev = m_ref[h]
    m_new = jnp.maximum(m_prev, jnp.max(s, axis=-1, keepdims=True))
    alpha = jnp.exp(m_prev - m_new)
    p = jnp.exp(s - m_new)
    l_ref[h] = alpha * l_ref[h] + jnp.sum(p, axis=-1, keepdims=True)
    acc_ref[:, hs] = alpha * acc_ref[:, hs] + jnp.dot(p.astype(jnp.bfloat16), v,
                                                      preferred_element_type=jnp.float32)
    m_ref[h] = m_new


def _attn_prompt_kernel(q_ref, k_ref, v_ref, cq_ref, ck_ref, o_ref, m_ref, l_ref, acc_ref, *, n_heads):
    qi, ki = pl.program_id(1), pl.program_id(2)
    tq, tk = q_ref.shape[0], k_ref.shape[0]
    scale = HEAD_DIM ** -0.5

    @pl.when(ki == 0)
    def _():
        m_ref[...] = jnp.full_like(m_ref, MASK_VALUE)
        l_ref[...] = jnp.zeros_like(l_ref)
        acc_ref[...] = jnp.zeros_like(acc_ref)

    @pl.when(ki <= qi)
    def _():
        q = q_ref[...].astype(jnp.bfloat16)
        k = k_ref[...].astype(jnp.bfloat16)
        v = v_ref[...].astype(jnp.bfloat16)
        cq = cq_ref[...]
        ck = ck_ref[...]
        q_pos = qi * tq + lax.broadcasted_iota(jnp.int32, (tq, tk), 0)
        k_pos = ki * tk + lax.broadcasted_iota(jnp.int32, (tq, tk), 1)
        mask = k_pos <= q_pos
        for h in range(n_heads):
            hs = slice(h * HEAD_DIM, (h + 1) * HEAD_DIM)
            s = _nt_dot(q[:, hs], k[:, hs]) * scale
            s = jnp.where(mask, s + cq[:, h:h + 1] - ck[h:h + 1, :], MASK_VALUE)
            _softmax_update(s, v[:, hs], m_ref, l_ref, acc_ref, h)

    @pl.when(ki == qi)
    def _():
        for h in range(n_heads):
            hs = slice(h * HEAD_DIM, (h + 1) * HEAD_DIM)
            o_ref[:, hs] = (acc_ref[:, hs] / l_ref[h]).astype(o_ref.dtype)


def attention_prompt(proj, batch, seq, col_q, d_attn, c_cols, c_rows):
    m = proj.shape[0]
    n_heads = d_attn // HEAD_DIM
    tq = _tile(seq, 512)
    nq = seq // tq
    cb = col_q // d_attn
    kv_row = lambda b, qi, ki: b * nq + jnp.minimum(ki, qi)
    return pl.pallas_call(
        functools.partial(_attn_prompt_kernel, n_heads=n_heads),
        grid=(batch, nq, nq),
        in_specs=[pl.BlockSpec((tq, d_attn), lambda b, qi, ki: (b * nq + qi, cb)),
                  pl.BlockSpec((tq, d_attn), lambda b, qi, ki: (kv_row(b, qi, ki), cb + 1)),
                  pl.BlockSpec((tq, d_attn), lambda b, qi, ki: (kv_row(b, qi, ki), cb + 2)),
                  pl.BlockSpec((tq, n_heads), lambda b, qi, ki: (b * nq + qi, 0)),
                  pl.BlockSpec((None, n_heads, tq), lambda b, qi, ki: (b, 0, jnp.minimum(ki, qi)))],
        out_specs=pl.BlockSpec((tq, d_attn), lambda b, qi, ki: (b * nq + qi, 0)),
        out_shape=jax.ShapeDtypeStruct((m, d_attn), jnp.bfloat16),
        scratch_shapes=[pltpu.VMEM((n_heads, tq, 1), jnp.float32),
                        pltpu.VMEM((n_heads, tq, 1), jnp.float32),
                        pltpu.VMEM((tq, d_attn), jnp.float32)],
        compiler_params=_cparams(("arbitrary", "arbitrary", "arbitrary")),
        name="attention_prompt",
    )(proj, proj, proj, c_cols, c_rows)


def _attn_sample_kernel(pt_ref, *refs, n_heads, pages_per_step, page):
    del pt_ref
    pps = pages_per_step
    q_ref, kn_ref, vn_ref = refs[0:3]
    k_refs = refs[3:3 + pps]
    v_refs = refs[3 + pps:3 + 2 * pps]
    ck_ref, cn_ref, o_ref, m_ref, l_ref, acc_ref = refs[3 + 2 * pps:]
    g = pl.program_id(1)
    steps = q_ref.shape[0]
    scale = HEAD_DIM ** -0.5

    @pl.when(g == 0)
    def _():
        m_ref[...] = jnp.full_like(m_ref, MASK_VALUE)
        l_ref[...] = jnp.zeros_like(l_ref)
        acc_ref[...] = jnp.zeros_like(acc_ref)

    q = q_ref[...].astype(jnp.bfloat16)
    cn = cn_ref[...]
    ck = ck_ref[...]
    sub = lax.broadcasted_iota(jnp.int32, (steps, LANES), 0)
    lane = lax.broadcasted_iota(jnp.int32, (steps, LANES), 1)

    def cq_of(h):
        return jnp.sum(jnp.where(sub == lane, cn[h:h + 1, :], 0.0), axis=1, keepdims=True)

    for h in range(n_heads):
        hs = slice(h * HEAD_DIM, (h + 1) * HEAD_DIM)
        k = jnp.concatenate([r[:, hs] for r in k_refs], axis=0).astype(jnp.bfloat16)
        v = jnp.concatenate([r[:, hs] for r in v_refs], axis=0).astype(jnp.bfloat16)
        s = _nt_dot(q[:, hs], k) * scale + cq_of(h) - ck[h:h + 1, :]
        _softmax_update(s, v, m_ref, l_ref, acc_ref, h)

    @pl.when(g == pl.num_programs(1) - 1)
    def _():
        pad = jnp.zeros((LANES - steps, HEAD_DIM), jnp.bfloat16)
        for h in range(n_heads):
            hs = slice(h * HEAD_DIM, (h + 1) * HEAD_DIM)
            k = jnp.concatenate([kn_ref[:, hs].astype(jnp.bfloat16), pad], axis=0)
            v = jnp.concatenate([vn_ref[:, hs].astype(jnp.bfloat16), pad], axis=0)
            s = _nt_dot(q[:, hs], k) * scale + cq_of(h) - cn[h:h + 1, :]
            s = jnp.where(lane <= sub, s, MASK_VALUE)
            _softmax_update(s, v, m_ref, l_ref, acc_ref, h)
            o_ref[:, hs] = (acc_ref[:, hs] / l_ref[h]).astype(o_ref.dtype)


def attention_sample(page_table_flat, n_pages, proj, batch, steps, col_q, d_attn, k_pool, v_pool, c_rows):
    m = proj.shape[0]
    n_heads = d_attn // HEAD_DIM
    page = k_pool.shape[1]
    pps = _tile(n_pages, 4)
    cb = col_q // d_attn

    def page_spec(i):
        return pl.BlockSpec((None, page, d_attn),
                            lambda b, g, pt: (pt[b * n_pages + g * pps + i], 0, 0))

    new_blk = n_pages * page // LANES
    in_specs = ([pl.BlockSpec((steps, d_attn), lambda b, g, pt: (b, cb)),
                 pl.BlockSpec((steps, d_attn), lambda b, g, pt: (b, cb + 1)),
                 pl.BlockSpec((steps, d_attn), lambda b, g, pt: (b, cb + 2))]
                + [page_spec(i) for i in range(pps)]
                + [page_spec(i) for i in range(pps)]
                + [pl.BlockSpec((None, n_heads, pps * page), lambda b, g, pt: (b, 0, g)),
                   pl.BlockSpec((None, n_heads, LANES), lambda b, g, pt: (b, 0, new_blk))])
    return pl.pallas_call(
        functools.partial(_attn_sample_kernel, n_heads=n_heads, pages_per_step=pps, page=page),
        grid_spec=pltpu.PrefetchScalarGridSpec(
            num_scalar_prefetch=1,
            grid=(batch, n_pages // pps),
            in_specs=in_specs,
            out_specs=pl.BlockSpec((steps, d_attn), lambda b, g, pt: (b, 0)),
            scratch_shapes=[pltpu.VMEM((n_heads, steps, 1), jnp.float32),
                            pltpu.VMEM((n_heads, steps, 1), jnp.float32),
                            pltpu.VMEM((steps, d_attn), jnp.float32)],
        ),
        out_shape=jax.ShapeDtypeStruct((m, d_attn), jnp.float32),
        compiler_params=_cparams(("arbitrary", "arbitrary")),
        name="attention_sample",
    )(page_table_flat, proj, proj, proj, *([k_pool] * pps), *([v_pool] * pps), c_rows, c_rows)


def _merge_kernel(ya_ref, yb_ref, wa_ref, wb_ref, ga_ref, gb_ref, z_ref):
    pa = jnp.dot(ya_ref[...].astype(jnp.bfloat16), wa_ref[...], preferred_element_type=jnp.float32)
    pb = jnp.dot(yb_ref[...].astype(jnp.bfloat16), wb_ref[...], preferred_element_type=jnp.float32)
    z = jax.nn.sigmoid(ga_ref[...]) * pa + jax.nn.sigmoid(gb_ref[...]) * pb
    z_ref[...] = z.astype(z_ref.dtype)


def branch_merge(ya, yb, w_branch, proj, col_gates):
    m, da = ya.shape
    db = yb.shape[1]
    d = w_branch.shape[2]
    tm = _tile(m, 512)
    tn = _tile(d, 1024)
    g0 = col_gates // tn
    g1 = (col_gates + d) // tn
    return pl.pallas_call(
        _merge_kernel,
        grid=(m // tm, d // tn),
        in_specs=[pl.BlockSpec((tm, da), lambda i, j: (i, 0)),
                  pl.BlockSpec((tm, db), lambda i, j: (i, 0)),
                  pl.BlockSpec((None, da, tn), lambda i, j: (0, 0, j)),
                  pl.BlockSpec((None, db, tn), lambda i, j: (1, 0, j)),
                  pl.BlockSpec((tm, tn), lambda i, j: (i, g0 + j)),
                  pl.BlockSpec((tm, tn), lambda i, j: (i, g1 + j))],
        out_specs=pl.BlockSpec((tm, tn), lambda i, j: (i, j)),
        out_shape=jax.ShapeDtypeStruct((m, d), jnp.bfloat16),
        compiler_params=_cparams(("arbitrary", "arbitrary")),
        name="branch_merge",
    )(ya, yb, w_branch, w_branch, proj, proj)


def _matmul_residual_kernel(a_ref, w_ref, x_ref, o_ref):
    o_ref[...] = x_ref[...] + jnp.dot(a_ref[...], w_ref[...], preferred_element_type=jnp.float32)


def matmul_residual(a, w, x):
    m, k = a.shape
    n = w.shape[1]
    pref = 1024 if k <= 2048 else 512
    tm = _tile(m, pref)
    tn = _tile(n, pref)
    return pl.pallas_call(
        _matmul_residual_kernel,
        grid=(m // tm, n // tn),
        in_specs=[pl.BlockSpec((tm, k), lambda i, j: (i, 0)),
                  pl.BlockSpec((k, tn), lambda i, j: (0, j)),
                  pl.BlockSpec((tm, tn), lambda i, j: (i, j))],
        out_specs=pl.BlockSpec((tm, tn), lambda i, j: (i, j)),
        out_shape=jax.ShapeDtypeStruct((m, n), jnp.float32),
        compiler_params=_cparams(("arbitrary", "arbitrary")),
        name="matmul_residual",
    )(a, w, x)


def _ffn_act(up_g, up_v, prev_g, prev_v, pos, cwg_ref, cbg_ref, cwv_ref, cbv_ref, width):
    gate = _causal_dwconv(up_g, prev_g, pos, cwg_ref, cbg_ref, width)
    val = _causal_dwconv(up_v, prev_v, pos, cwv_ref, cbv_ref, width)
    return jax.nn.gelu(gate) * val


def _ffn_up_prompt_kernel(xn_ref, wg_ref, wv_ref, cwg_ref, cbg_ref, cwv_ref, cbv_ref,
                          h_ref, tg_ref, tv_ref, pg_ref, pv_ref, *, width):
    @pl.when(pl.program_id(2) == 0)
    def _():
        pg_ref[...] = jnp.zeros_like(pg_ref)
        pv_ref[...] = jnp.zeros_like(pv_ref)

    xn = xn_ref[...]
    up_g = jnp.dot(xn, wg_ref[...], preferred_element_type=jnp.float32)
    up_v = jnp.dot(xn, wv_ref[...], preferred_element_type=jnp.float32)
    pos = lax.broadcasted_iota(jnp.int32, up_g.shape, 0)
    act = _ffn_act(up_g, up_v, pg_ref[...], pv_ref[...], pos, cwg_ref, cbg_ref, cwv_ref, cbv_ref, width)
    h_ref[...] = act.astype(h_ref.dtype)
    rows = up_g.shape[0]
    tg_ref[...] = up_g[rows - SUBLANES:, :]
    tv_ref[...] = up_v[rows - SUBLANES:, :]
    pg_ref[0:SUBLANES, :] = _tail_rows(up_g, width)
    pv_ref[0:SUBLANES, :] = _tail_rows(up_v, width)


def _ffn_up_sample_kernel(xn_ref, wg_ref, wv_ref, cwg_ref, cbg_ref, cwv_ref, cbv_ref, pg_ref, pv_ref,
                          h_ref, ug_ref, uv_ref, *, width, seq):
    xn = xn_ref[...]
    up_g = jnp.dot(xn, wg_ref[...], preferred_element_type=jnp.float32)
    up_v = jnp.dot(xn, wv_ref[...], preferred_element_type=jnp.float32)
    pos = lax.broadcasted_iota(jnp.int32, up_g.shape, 0) % seq
    act = _ffn_act(up_g, up_v, pg_ref[...], pv_ref[...], pos, cwg_ref, cbg_ref, cwv_ref, cbv_ref, width)
    h_ref[...] = act.astype(h_ref.dtype)
    ug_ref[...] = up_g
    uv_ref[...] = up_v


def ffn_up_prompt(xn, batch, seq, w_up, conv_w, conv_b):
    m, d = xn.shape
    f = w_up.shape[1] // 2
    width = conv_w.shape[0]
    tt = _tile(seq, 512)
    tn = _tile(f, 512)
    nt, nf = seq // tt, f // tn
    gcol = lambda n, b, t: (0, n)
    vcol = lambda n, b, t: (0, nf + n)
    h, tg, tv = pl.pallas_call(
        functools.partial(_ffn_up_prompt_kernel, width=width),
        grid=(nf, batch, nt),
        in_specs=[pl.BlockSpec((tt, d), lambda n, b, t: (b * nt + t, 0)),
                  pl.BlockSpec((d, tn), gcol),
                  pl.BlockSpec((d, tn), vcol),
                  pl.BlockSpec((width, tn), gcol),
                  pl.BlockSpec((1, tn), gcol),
                  pl.BlockSpec((width, tn), vcol),
                  pl.BlockSpec((1, tn), vcol)],
        out_specs=[pl.BlockSpec((tt, tn), lambda n, b, t: (b * nt + t, n)),
                   pl.BlockSpec((None, SUBLANES, tn), lambda n, b, t: (b, 0, n)),
                   pl.BlockSpec((None, SUBLANES, tn), lambda n, b, t: (b, 0, n))],
        out_shape=[jax.ShapeDtypeStruct((m, f), jnp.bfloat16),
                   jax.ShapeDtypeStruct((batch, SUBLANES, f), jnp.float32),
                   jax.ShapeDtypeStruct((batch, SUBLANES, f), jnp.float32)],
        scratch_shapes=[pltpu.VMEM((tt, tn), jnp.float32),
                        pltpu.VMEM((tt, tn), jnp.float32)],
        compiler_params=_cparams(("arbitrary", "arbitrary", "arbitrary")),
        name="ffn_up_prompt",
    )(xn, w_up, w_up, conv_w, conv_b, conv_w, conv_b)
    return h, tg, tv


def ffn_up_sample(xn, seq, w_up, conv_w, conv_b, prev_rows):
    m, d = xn.shape
    f = w_up.shape[1] // 2
    width = conv_w.shape[0]
    tt = _tile(m, 512)
    tn = _tile(f, 512)
    nf = f // tn
    gcol = lambda n, t: (0, n)
    vcol = lambda n, t: (0, nf + n)
    return pl.pallas_call(
        functools.partial(_ffn_up_sample_kernel, width=width, seq=seq),
        grid=(nf, m // tt),
        in_specs=[pl.BlockSpec((tt, d), lambda n, t: (t, 0)),
                  pl.BlockSpec((d, tn), gcol),
                  pl.BlockSpec((d, tn), vcol),
                  pl.BlockSpec((width, tn), gcol),
                  pl.BlockSpec((1, tn), gcol),
                  pl.BlockSpec((width, tn), vcol),
                  pl.BlockSpec((1, tn), vcol),
                  pl.BlockSpec((tt, tn), lambda n, t: (t, n)),
                  pl.BlockSpec((tt, tn), lambda n, t: (t, nf + n))],
        out_specs=[pl.BlockSpec((tt, tn), lambda n, t: (t, n)),
                   pl.BlockSpec((tt, tn), lambda n, t: (t, n)),
                   pl.BlockSpec((tt, tn), lambda n, t: (t, n))],
        out_shape=[jax.ShapeDtypeStruct((m, f), jnp.bfloat16),
                   jax.ShapeDtypeStruct((m, f), jnp.float32),
                   jax.ShapeDtypeStruct((m, f), jnp.float32)],
        compiler_params=_cparams(("arbitrary", "arbitrary")),
        name="ffn_up_sample",
    )(xn, w_up, w_up, conv_w, conv_b, conv_w, conv_b, prev_rows, prev_rows)


def _state_rows(state, seq):
    b, w, c = state.shape
    return jnp.pad(state, ((0, 0), (0, seq - w), (0, 0))).reshape(b * seq, c)


def kernel(x_prompt, x_sample, cache_k, cache_v, cache_logf, page_table, state_rnn_h, state_rnn_conv, state_ffn_conv, norm_mix_g, w_in, rnn_conv_w, rnn_conv_b, rg_w_a, rg_b_a, rg_w_x, rg_b_x, rg_lambda, fox_b_f, w_branch, w_out, norm_ffn_g, w_up, ffn_conv_w, ffn_conv_b, w_down, norm_final_g):
    bp, seq, d = x_prompt.shape
    bs, steps, _ = x_sample.shape
    depth = w_in.shape[0]
    d_rnn = rg_lambda.shape[1]
    n_heads = fox_b_f.shape[1]
    d_attn = n_heads * HEAD_DIM
    n_pages = page_table.shape[1]
    page = cache_k.shape[2]
    n_pool = cache_k.shape[1]
    f = w_down.shape[1]
    assert steps == SUBLANES and d_rnn == d_attn
    col_q = 2 * d_rnn
    col_f = col_q + 3 * d_attn
    col_gates = col_f + n_heads
    bf16 = jnp.bfloat16

    xp = x_prompt.reshape(bp * seq, d)
    xs = x_sample.reshape(bs * steps, d)
    pt_flat = page_table.reshape(-1)
    k_pool = cache_k.reshape(depth, n_pool, page, d_attn)
    v_pool = cache_v.reshape(depth, n_pool, page, d_attn)
    lf_pool_rows = jnp.swapaxes(cache_logf, 2, 3)

    outs = {name: [] for name in ("kp", "vp", "lfp", "hp", "rcp", "fcp", "ks", "vs", "lfs", "hs", "rcs", "fcs")}
    for l in range(depth):
        w_main = jnp.concatenate([w_in[l][:, :col_f], w_in[l][:, col_gates:]], axis=1).astype(bf16)
        w_f = jnp.pad(w_in[l][:, col_f:col_gates], ((0, 0), (0, LANES - n_heads))).astype(bf16)
        b_f = jnp.pad(fox_b_f[l], (0, LANES - n_heads)).reshape(1, LANES)
        wa = rg_w_a[l].astype(bf16)
        wx = rg_w_x[l].astype(bf16)
        vec = lambda a: a.reshape(1, -1)
        rg = (rnn_conv_w[l], vec(rnn_conv_b[l]), wa, wx, vec(rg_b_a[l]), vec(rg_b_x[l]), vec(rg_lambda[l]))
        wb = w_branch[l].astype(bf16)
        wo = w_out[l].astype(bf16)
        wu = w_up[l].astype(bf16)
        wd = w_down[l].astype(bf16)
        fcb = vec(ffn_conv_b[l])
        main_gates = col_f

        proj, logf = in_projection(xp, norm_mix_g[l], w_main, w_f, b_f)
        ya, h_last = rglru_prompt(proj, bp, seq, d_rnn, *rg)
        lf = logf[:, :n_heads]
        c_rows = cumsum_prompt(jnp.swapaxes(lf.reshape(bp, seq, n_heads), 1, 2))
        c_cols = jnp.swapaxes(c_rows, 1, 2).reshape(bp * seq, n_heads)
        yb = attention_prompt(proj, bp, seq, col_q, d_attn, c_cols, c_rows)
        z = branch_merge(ya, yb, wb, proj, main_gates)
        xp = matmul_residual(z, wo, xp)
        hid, tail_g, tail_v = ffn_up_prompt(rmsnorm(xp, norm_ffn_g[l], bf16), bp, seq, wu, ffn_conv_w[l], fcb)
        xp = matmul_residual(hid, wd, xp)
        proj3 = proj.reshape(bp, seq, -1)
        outs["kp"].append(proj3[:, :, col_q + d_attn:col_q + 2 * d_attn].reshape(bp, seq, n_heads, HEAD_DIM))
        outs["vp"].append(proj3[:, :, col_q + 2 * d_attn:col_q + 3 * d_attn].reshape(bp, seq, n_heads, HEAD_DIM))
        outs["lfp"].append(lf.reshape(bp, seq, n_heads))
        outs["hp"].append(h_last)
        outs["rcp"].append(proj3[:, seq - (rnn_conv_w.shape[1] - 1):, :d_rnn])
        fw = ffn_conv_w.shape[1] - 1
        outs["fcp"].append(jnp.concatenate([tail_g[:, SUBLANES - fw:], tail_v[:, SUBLANES - fw:]], axis=-1))

        proj, logf = in_projection(xs, norm_mix_g[l], w_main, w_f, b_f)
        ya, h_last = rglru_sample(proj, bs, steps, d_rnn, *rg,
                                  _state_rows(state_rnn_conv[l], steps),
                                  jnp.repeat(state_rnn_h[l], steps, axis=0))
        lf = logf[:, :n_heads]
        lf_new_rows = jnp.pad(jnp.swapaxes(lf.reshape(bs, steps, n_heads), 1, 2),
                              ((0, 0), (0, 0), (0, LANES - steps)))
        c_rows = cumsum_sample(pt_flat, n_pages, lf_pool_rows[l], lf_new_rows)
        yb = attention_sample(pt_flat, n_pages, proj, bs, steps, col_q, d_attn, k_pool[l], v_pool[l], c_rows)
        z = branch_merge(ya, yb, wb, proj, main_gates)
        xs = matmul_residual(z, wo, xs)
        hid, up_g, up_v = ffn_up_sample(rmsnorm(xs, norm_ffn_g[l], bf16), steps, wu, ffn_conv_w[l], fcb,
                                        _state_rows(state_ffn_conv[l], steps))
        xs = matmul_residual(hid, wd, xs)
        proj3 = proj.reshape(bs, steps, -1)
        outs["ks"].append(proj3[:, :, col_q + d_attn:col_q + 2 * d_attn].reshape(bs, steps, n_heads, HEAD_DIM))
        outs["vs"].append(proj3[:, :, col_q + 2 * d_attn:col_q + 3 * d_attn].reshape(bs, steps, n_heads, HEAD_DIM))
        outs["lfs"].append(lf.reshape(bs, steps, n_heads))
        outs["hs"].append(h_last)
        outs["rcs"].append(proj3[:, steps - (rnn_conv_w.shape[1] - 1):, :d_rnn])
        up = jnp.concatenate([up_g, up_v], axis=-1).reshape(bs, steps, 2 * f)
        outs["fcs"].append(up[:, steps - fw:])

    y_prompt = rmsnorm(xp, norm_final_g, jnp.float32).reshape(bp, seq, d)
    y_sample = rmsnorm(xs, norm_final_g, jnp.float32).reshape(bs, steps, d)
    st = lambda name: jnp.stack(outs[name])
    return (y_prompt, y_sample, st("kp"), st("vp"), st("lfp"), st("hp"), st("rcp"), st("fcp"),
            st("ks"), st("vs"), st("lfs"), st("hs"), st("rcs"), st("fcs"))
```

```python
import functools

import jax
import jax.numpy as jnp
from jax import lax
from jax.experimental import pallas as pl
from jax.experimental.pallas import tpu as pltpu

HEAD_DIM = 128
RNN_BLOCK_W = 128
RG_C = 8.0
NORM_EPS = 1e-6
SUBLANES = 8
LANES = 128
MASK_VALUE = -1e30
LOG2E = 1.4426950408889634
VMEM_LIMIT_BYTES = 56 * 1024 * 1024


def _cparams(semantics):
    return pltpu.CompilerParams(dimension_semantics=semantics, vmem_limit_bytes=VMEM_LIMIT_BYTES)


def _tile(dim, pref):
    t = min(dim, pref)
    assert dim % t == 0, (dim, pref)
    return t


def _nt_dot(a, b):
    return lax.dot_general(a, b, (((1,), (1,)), ((), ())), preferred_element_type=jnp.float32)


def _rmsnorm_rows(x, g):
    inv = lax.rsqrt(jnp.mean(x * x, axis=-1, keepdims=True) + NORM_EPS)
    return x * inv * g


def _rmsnorm_kernel(x_ref, g_ref, o_ref):
    o_ref[...] = _rmsnorm_rows(x_ref[...], g_ref[...]).astype(o_ref.dtype)


def rmsnorm(x, g, out_dtype):
    m, d = x.shape
    tm = _tile(m, 512)
    return pl.pallas_call(
        _rmsnorm_kernel,
        grid=(m // tm,),
        in_specs=[pl.BlockSpec((tm, d), lambda i: (i, 0)),
                  pl.BlockSpec((1, d), lambda i: (0, 0))],
        out_specs=pl.BlockSpec((tm, d), lambda i: (i, 0)),
        out_shape=jax.ShapeDtypeStruct((m, d), out_dtype),
        compiler_params=_cparams(("arbitrary",)),
        name="rmsnorm",
    )(x, g.reshape(1, d))


def _inproj_kernel(x_ref, g_ref, w_ref, wf_ref, bf_ref, proj_ref, logf_ref, xn_ref):
    @pl.when(pl.program_id(1) == 0)
    def _():
        xn = _rmsnorm_rows(x_ref[...], g_ref[...]).astype(jnp.bfloat16)
        xn_ref[...] = xn
        f = jnp.dot(xn, wf_ref[...], preferred_element_type=jnp.float32)
        logf_ref[...] = jax.nn.log_sigmoid(f + bf_ref[...])

    proj_ref[...] = jnp.dot(xn_ref[...], w_ref[...], preferred_element_type=jnp.float32)


def in_projection(x, g, w_main, w_f, b_f):
    m, d = x.shape
    n = w_main.shape[1]
    tm = _tile(m, 1024)
    tn = _tile(n, 1024)
    return pl.pallas_call(
        _inproj_kernel,
        grid=(m // tm, n // tn),
        in_specs=[pl.BlockSpec((tm, d), lambda i, j: (i, 0)),
                  pl.BlockSpec((1, d), lambda i, j: (0, 0)),
                  pl.BlockSpec((d, tn), lambda i, j: (0, j)),
                  pl.BlockSpec((d, LANES), lambda i, j: (0, 0)),
                  pl.BlockSpec((1, LANES), lambda i, j: (0, 0))],
        out_specs=[pl.BlockSpec((tm, tn), lambda i, j: (i, j)),
                   pl.BlockSpec((tm, LANES), lambda i, j: (i, 0))],
        out_shape=[jax.ShapeDtypeStruct((m, n), jnp.float32),
                   jax.ShapeDtypeStruct((m, LANES), jnp.float32)],
        scratch_shapes=[pltpu.VMEM((tm, d), jnp.bfloat16)],
        compiler_params=_cparams(("arbitrary", "arbitrary")),
        name="in_projection",
    )(x, g.reshape(1, d), w_main, w_f, b_f)


def _causal_dwconv(x, prev, pos, cw_ref, cb_ref, width):
    rows = x.shape[0]
    y = cb_ref[...] + cw_ref[width - 1:width, :] * x
    for d in range(1, width):
        up = width - 1 - d
        before = prev if up == 0 else pltpu.roll(prev, rows - up, axis=0)
        shifted = jnp.where(pos >= d, pltpu.roll(x, d, axis=0), before)
        y = y + cw_ref[width - 1 - d:width - d, :] * shifted
    return y


def _tail_rows(x, width):
    rows = x.shape[0]
    return pltpu.roll(x[rows - SUBLANES:, :], width - 1, axis=0)


def _rglru_core(x, xg, prev, pos, period, h_in, cw_ref, cb_ref, wa_ref, wx_ref, ba_ref, bx_ref, lam_ref, width):
    tc = x.shape[1]
    xc = _causal_dwconv(x, prev, pos, cw_ref, cb_ref, width)
    ga, gx = [], []
    for nb in range(tc // RNN_BLOCK_W):
        xb = xc[:, nb * RNN_BLOCK_W:(nb + 1) * RNN_BLOCK_W].astype(jnp.bfloat16)
        ga.append(jnp.dot(xb, wa_ref[nb], preferred_element_type=jnp.float32))
        gx.append(jnp.dot(xb, wx_ref[nb], preferred_element_type=jnp.float32))
    r_gate = jax.nn.sigmoid(jnp.concatenate(ga, axis=1) + ba_ref[...])
    i_gate = jax.nn.sigmoid(jnp.concatenate(gx, axis=1) + bx_ref[...])
    log_a = -RG_C * r_gate * jax.nn.softplus(-lam_ref[...])
    a = jnp.exp(log_a)
    u = jnp.sqrt(-jnp.tanh(log_a) * (a * a + 1.0)) * (i_gate * xc)
    s = 1
    while s < period:
        keep = pos >= s
        a_prev = jnp.where(keep, pltpu.roll(a, s, axis=0), 1.0)
        u_prev = jnp.where(keep, pltpu.roll(u, s, axis=0), 0.0)
        u = a * u_prev + u
        a = a * a_prev
        s *= 2
    h = a * h_in + u
    return h, h * jax.nn.gelu(xg)


def _rglru_prompt_kernel(xr_ref, xg_ref, cw_ref, cb_ref, wa_ref, wx_ref, ba_ref, bx_ref, lam_ref,
                         ya_ref, hl_ref, prev_ref, hc_ref, *, width):
    tt = xr_ref.shape[0]

    @pl.when(pl.program_id(2) == 0)
    def _():
        prev_ref[...] = jnp.zeros_like(prev_ref)
        hc_ref[...] = jnp.zeros_like(hc_ref)

    x = xr_ref[...]
    pos = lax.broadcasted_iota(jnp.int32, x.shape, 0)
    h_in = hc_ref[SUBLANES - 1:SUBLANES, :]
    h, ya = _rglru_core(x, xg_ref[...], prev_ref[...], pos, tt, h_in, cw_ref, cb_ref, wa_ref, wx_ref,
                        ba_ref, bx_ref, lam_ref, width)
    ya_ref[...] = ya.astype(ya_ref.dtype)
    hc_ref[...] = h[tt - SUBLANES:, :]
    hl_ref[...] = h[tt - 1:tt, :]
    prev_ref[0:SUBLANES, :] = _tail_rows(x, width)


def _rglru_sample_kernel(xr_ref, xg_ref, cw_ref, cb_ref, wa_ref, wx_ref, ba_ref, bx_ref, lam_ref,
                         prev_ref, h0_ref, ya_ref, hl_ref, hs_ref, *, width, seq):
    x = xr_ref[...]
    pos = lax.broadcasted_iota(jnp.int32, x.shape, 0) % seq
    h, ya = _rglru_core(x, xg_ref[...], prev_ref[...], pos, seq, h0_ref[...], cw_ref, cb_ref, wa_ref,
                        wx_ref, ba_ref, bx_ref, lam_ref, width)
    ya_ref[...] = ya.astype(ya_ref.dtype)
    for j in range(h.shape[1] // LANES):
        hs_ref[j] = h[:, j * LANES:(j + 1) * LANES]
        hl_ref[:, j * LANES:(j + 1) * LANES] = hs_ref[j, pl.ds(seq - 1, hl_ref.shape[0], stride=seq), :]


def rglru_prompt(proj, batch, seq, d_rnn, conv_w, conv_b, w_a, w_x, b_a, b_x, lam):
    m = proj.shape[0]
    width = conv_w.shape[0]
    tt = _tile(seq, 512)
    tc = _tile(d_rnn, 256)
    nt, nc = seq // tt, d_rnn // tc
    nb = tc // RNN_BLOCK_W
    row = lambda c, b, t: (b * nt + t, c)
    vec = lambda c, b, t: (0, c)
    ya, hl = pl.pallas_call(
        functools.partial(_rglru_prompt_kernel, width=width),
        grid=(nc, batch, nt),
        in_specs=[pl.BlockSpec((tt, tc), row),
                  pl.BlockSpec((tt, tc), lambda c, b, t: (b * nt + t, nc + c)),
                  pl.BlockSpec((width, tc), vec),
                  pl.BlockSpec((1, tc), vec),
                  pl.BlockSpec((nb, RNN_BLOCK_W, RNN_BLOCK_W), lambda c, b, t: (c, 0, 0)),
                  pl.BlockSpec((nb, RNN_BLOCK_W, RNN_BLOCK_W), lambda c, b, t: (c, 0, 0)),
                  pl.BlockSpec((1, tc), vec),
                  pl.BlockSpec((1, tc), vec),
                  pl.BlockSpec((1, tc), vec)],
        out_specs=[pl.BlockSpec((tt, tc), row),
                   pl.BlockSpec((None, 1, tc), lambda c, b, t: (b, 0, c))],
        out_shape=[jax.ShapeDtypeStruct((m, d_rnn), jnp.bfloat16),
                   jax.ShapeDtypeStruct((batch, 1, d_rnn), jnp.float32)],
        scratch_shapes=[pltpu.VMEM((tt, tc), jnp.float32),
                        pltpu.VMEM((SUBLANES, tc), jnp.float32)],
        compiler_params=_cparams(("arbitrary", "arbitrary", "arbitrary")),
        name="rglru_prompt",
    )(proj, proj, conv_w, conv_b, w_a, w_x, b_a, b_x, lam)
    return ya, hl.reshape(batch, d_rnn)


def rglru_sample(proj, batch, seq, d_rnn, conv_w, conv_b, w_a, w_x, b_a, b_x, lam, prev_rows, h0_rows):
    m = proj.shape[0]
    width = conv_w.shape[0]
    tt = _tile(m, 512)
    tc = _tile(d_rnn, 256)
    nc = d_rnn // tc
    nb = tc // RNN_BLOCK_W
    row = lambda c, t: (t, c)
    vec = lambda c, t: (0, c)
    ya, hl = pl.pallas_call(
        functools.partial(_rglru_sample_kernel, width=width, seq=seq),
        grid=(nc, m // tt),
        in_specs=[pl.BlockSpec((tt, tc), row),
                  pl.BlockSpec((tt, tc), lambda c, t: (t, nc + c)),
                  pl.BlockSpec((width, tc), vec),
                  pl.BlockSpec((1, tc), vec),
                  pl.BlockSpec((nb, RNN_BLOCK_W, RNN_BLOCK_W), lambda c, t: (c, 0, 0)),
                  pl.BlockSpec((nb, RNN_BLOCK_W, RNN_BLOCK_W), lambda c, t: (c, 0, 0)),
                  pl.BlockSpec((1, tc), vec),
                  pl.BlockSpec((1, tc), vec),
                  pl.BlockSpec((1, tc), vec),
                  pl.BlockSpec((tt, tc), row),
                  pl.BlockSpec((tt, tc), row)],
        out_specs=[pl.BlockSpec((tt, tc), row),
                   pl.BlockSpec((tt // seq, tc), row)],
        out_shape=[jax.ShapeDtypeStruct((m, d_rnn), jnp.bfloat16),
                   jax.ShapeDtypeStruct((batch, d_rnn), jnp.float32)],
        scratch_shapes=[pltpu.VMEM((tc // LANES, tt, LANES), jnp.float32)],
        compiler_params=_cparams(("arbitrary", "arbitrary")),
        name="rglru_sample",
    )(proj, proj, conv_w, conv_b, w_a, w_x, b_a, b_x, lam, prev_rows, h0_rows)
    return ya, hl


def _lane_cumsum(x):
    lane = lax.broadcasted_iota(jnp.int32, x.shape, 1)
    s = 1
    while s < x.shape[1]:
        x = x + jnp.where(lane >= s, pltpu.roll(x, s, axis=1), 0.0)
        s *= 2
    return x


def _cumsum_prompt_kernel(lf_ref, c_ref):
    c_ref[...] = _lane_cumsum(lf_ref[...])


def _cumsum_sample_kernel(pt_ref, *refs):
    del pt_ref
    c_ref = refs[-1]
    c_ref[...] = _lane_cumsum(jnp.concatenate([r[...] for r in refs[:-1]], axis=1))


def cumsum_prompt(lf_rows):
    b, h, t = lf_rows.shape
    return pl.pallas_call(
        _cumsum_prompt_kernel,
        grid=(b,),
        in_specs=[pl.BlockSpec((None, h, t), lambda i: (i, 0, 0))],
        out_specs=pl.BlockSpec((None, h, t), lambda i: (i, 0, 0)),
        out_shape=jax.ShapeDtypeStruct((b, h, t), jnp.float32),
        compiler_params=_cparams(("arbitrary",)),
        name="cumsum_prompt",
    )(lf_rows)


def cumsum_sample(page_table_flat, n_pages, layer, lf_pool_rows, lf_new_rows):
    b, h, tn = lf_new_rows.shape
    page = lf_pool_rows.shape[3]
    total = n_pages * page + tn

    def page_spec(i):
        return pl.BlockSpec((None, None, h, page), lambda s, pt: (layer, pt[s * n_pages + i], 0, 0))

    return pl.pallas_call(
        _cumsum_sample_kernel,
        grid_spec=pltpu.PrefetchScalarGridSpec(
            num_scalar_prefetch=1,
            grid=(b,),
            in_specs=[page_spec(i) for i in range(n_pages)]
            + [pl.BlockSpec((None, h, tn), lambda s, pt: (s, 0, 0))],
            out_specs=pl.BlockSpec((None, h, total), lambda s, pt: (s, 0, 0)),
        ),
        out_shape=jax.ShapeDtypeStruct((b, h, total), jnp.float32),
        compiler_params=_cparams(("arbitrary",)),
        name="cumsum_sample",
    )(page_table_flat, *([lf_pool_rows] * n_pages), lf_new_rows)


def _attn_prompt_kernel(q_ref, k_ref, v_ref, cq_ref, ck_ref, o_ref, q2_ref, cqw_ref, m_ref, l_ref, acc_ref,
                        *, n_heads):
    qi, ki = pl.program_id(1), pl.program_id(2)
    tq, tk = q_ref.shape[0], k_ref.shape[0]

    wide = (tq, LANES)

    @pl.when(ki == 0)
    def _():
        q2_ref[...] = (q_ref[...] * (HEAD_DIM ** -0.5 * LOG2E)).astype(jnp.bfloat16)
        cq2 = cq_ref[...] * LOG2E
        for h in range(n_heads):
            cqw_ref[h] = jnp.broadcast_to(cq2[:, h:h + 1], wide)
        m_ref[...] = jnp.full_like(m_ref, MASK_VALUE)
        l_ref[...] = jnp.zeros_like(l_ref)
        acc_ref[...] = jnp.zeros_like(acc_ref)

    def step(diagonal):
        k = k_ref[...].astype(jnp.bfloat16)
        v = v_ref[...].astype(jnp.bfloat16)
        ck2 = ck_ref[...] * LOG2E
        if diagonal:
            causal = (lax.broadcasted_iota(jnp.int32, (tq, tk), 1)
                      <= lax.broadcasted_iota(jnp.int32, (tq, tk), 0))
        for h in range(n_heads):
            hs = slice(h * HEAD_DIM, (h + 1) * HEAD_DIM)
            t = _nt_dot(q2_ref[:, hs], k[:, hs]) - ck2[h:h + 1, :]
            if diagonal:
                t = jnp.where(causal, t, MASK_VALUE)
            cq_h = cqw_ref[h]
            m_prev = m_ref[h]
            m_new = jnp.maximum(m_prev, jnp.broadcast_to(jnp.max(t, axis=-1, keepdims=True), wide) + cq_h)
            alpha = jnp.exp2(m_prev - m_new)
            shift = cq_h - m_new
            p = jnp.exp2(t + jnp.concatenate([shift] * (tk // LANES), axis=1))
            l_ref[h] = alpha * l_ref[h] + jnp.broadcast_to(jnp.sum(p, axis=-1, keepdims=True), wide)
            acc_ref[:, hs] = alpha * acc_ref[:, hs] + jnp.dot(p.astype(jnp.bfloat16), v[:, hs],
                                                              preferred_element_type=jnp.float32)
            m_ref[h] = m_new

    @pl.when(ki < qi)
    def _():
        step(False)

    @pl.when(ki == qi)
    def _():
        step(True)
        for h in range(n_heads):
            hs = slice(h * HEAD_DIM, (h + 1) * HEAD_DIM)
            o_ref[:, hs] = (acc_ref[:, hs] / l_ref[h]).astype(o_ref.dtype)


def attention_prompt(proj, batch, seq, col_q, d_attn, c_cols, c_rows):
    m = proj.shape[0]
    n_heads = d_attn // HEAD_DIM
    tq = _tile(seq, 512)
    nq = seq // tq
    cb = col_q // d_attn
    kv_row = lambda b, qi, ki: b * nq + jnp.minimum(ki, qi)
    return pl.pallas_call(
        functools.partial(_attn_prompt_kernel, n_heads=n_heads),
        grid=(batch, nq, nq),
        in_specs=[pl.BlockSpec((tq, d_attn), lambda b, qi, ki: (b * nq + qi, cb)),
                  pl.BlockSpec((tq, d_attn), lambda b, qi, ki: (kv_row(b, qi, ki), cb + 1)),
                  pl.BlockSpec((tq, d_attn), lambda b, qi, ki: (kv_row(b, qi, ki), cb + 2)),
                  pl.BlockSpec((tq, n_heads), lambda b, qi, ki: (b * nq + qi, 0)),
                  pl.BlockSpec((None, n_heads, tq), lambda b, qi, ki: (b, 0, jnp.minimum(ki, qi)))],
        out_specs=pl.BlockSpec((tq, d_attn), lambda b, qi, ki: (b * nq + qi, 0)),
        out_shape=jax.ShapeDtypeStruct((m, d_attn), jnp.bfloat16),
        scratch_shapes=[pltpu.VMEM((tq, d_attn), jnp.bfloat16),
                        pltpu.VMEM((n_heads, tq, LANES), jnp.float32),
                        pltpu.VMEM((n_heads, tq, LANES), jnp.float32),
                        pltpu.VMEM((n_heads, tq, LANES), jnp.float32),
                        pltpu.VMEM((tq, d_attn), jnp.float32)],
        compiler_params=_cparams(("arbitrary", "arbitrary", "arbitrary")),
        name="attention_prompt",
    )(proj, proj, proj, c_cols, c_rows)


def _attn_sample_kernel(pt_ref, *refs, n_heads, pages_per_step):
    del pt_ref
    pps = pages_per_step
    q_ref, kn_ref, vn_ref = refs[0:3]
    k_refs = refs[3:3 + pps]
    v_refs = refs[3 + pps:3 + 2 * pps]
    ck_ref, cn_ref, o_ref, m_ref, l_ref, acc_ref = refs[3 + 2 * pps:]
    g = pl.program_id(1)
    steps = q_ref.shape[0]
    page = k_refs[0].shape[0] // n_heads
    scale = HEAD_DIM ** -0.5

    @pl.when(g == 0)
    def _():
        m_ref[...] = jnp.full_like(m_ref, MASK_VALUE)
        l_ref[...] = jnp.zeros_like(l_ref)
        acc_ref[...] = jnp.zeros_like(acc_ref)

    q = q_ref[...].astype(jnp.bfloat16)
    cn = cn_ref[...]
    ck = ck_ref[...]
    sub = lax.broadcasted_iota(jnp.int32, (steps, LANES), 0)
    lane = lax.broadcasted_iota(jnp.int32, (steps, LANES), 1)
    cq = jnp.concatenate(
        [jnp.sum(jnp.where(sub == lane, cn[h:h + 1, :], 0.0), axis=1, keepdims=True) for h in range(n_heads)],
        axis=0)

    def rows_of(c):
        return jnp.concatenate([jnp.broadcast_to(c[h:h + 1, :], (steps, c.shape[1])) for h in range(n_heads)],
                               axis=0)

    def head_of(page_refs, h):
        return jnp.concatenate([r[pl.ds(h, page, stride=n_heads), :] for r in page_refs],
                               axis=0).astype(jnp.bfloat16)

    def update(s, v_of):
        m_prev = m_ref[...]
        m_new = jnp.maximum(m_prev, jnp.max(s, axis=-1, keepdims=True))
        alpha = jnp.exp(m_prev - m_new)
        p = jnp.exp(s - m_new)
        l_ref[...] = alpha * l_ref[...] + jnp.sum(p, axis=-1, keepdims=True)
        m_ref[...] = m_new
        pv = [jnp.dot(p[h * steps:(h + 1) * steps, :].astype(jnp.bfloat16), v_of(h),
                      preferred_element_type=jnp.float32) for h in range(n_heads)]
        acc_ref[...] = alpha * acc_ref[...] + jnp.concatenate(pv, axis=0)

    s = jnp.concatenate(
        [_nt_dot(q[:, h * HEAD_DIM:(h + 1) * HEAD_DIM], head_of(k_refs, h)) for h in range(n_heads)], axis=0)
    update(s * scale + cq - rows_of(ck), lambda h: head_of(v_refs, h))

    @pl.when(g == pl.num_programs(1) - 1)
    def _():
        pad = jnp.zeros((LANES - steps, HEAD_DIM), jnp.bfloat16)

        def new_of(ref, h):
            return jnp.concatenate([ref[:, h * HEAD_DIM:(h + 1) * HEAD_DIM].astype(jnp.bfloat16), pad], axis=0)

        s_new = jnp.concatenate(
            [_nt_dot(q[:, h * HEAD_DIM:(h + 1) * HEAD_DIM], new_of(kn_ref, h)) for h in range(n_heads)], axis=0)
        causal = jnp.concatenate([lane <= sub] * n_heads, axis=0)
        update(jnp.where(causal, s_new * scale + cq - rows_of(cn), MASK_VALUE), lambda h: new_of(vn_ref, h))
        out = acc_ref[...] / l_ref[...]
        for h in range(n_heads):
            o_ref[:, h * HEAD_DIM:(h + 1) * HEAD_DIM] = out[h * steps:(h + 1) * steps, :].astype(o_ref.dtype)


def attention_sample(page_table_flat, n_pages, layer, proj, batch, steps, col_q, d_attn, k_pool, v_pool, c_rows):
    m = proj.shape[0]
    n_heads = d_attn // HEAD_DIM
    page = k_pool.shape[2] // n_heads
    pps = _tile(n_pages, 8)
    cb = col_q // d_attn

    def page_spec(i):
        return pl.BlockSpec((None, None, page * n_heads, HEAD_DIM),
                            lambda b, g, pt: (layer, pt[b * n_pages + g * pps + i], 0, 0))

    new_blk = n_pages * page // LANES
    in_specs = ([pl.BlockSpec((steps, d_attn), lambda b, g, pt: (b, cb)),
                 pl.BlockSpec((steps, d_attn), lambda b, g, pt: (b, cb + 1)),
                 pl.BlockSpec((steps, d_attn), lambda b, g, pt: (b, cb + 2))]
                + [page_spec(i) for i in range(pps)]
                + [page_spec(i) for i in range(pps)]
                + [pl.BlockSpec((None, n_heads, pps * page), lambda b, g, pt: (b, 0, g)),
                   pl.BlockSpec((None, n_heads, LANES), lambda b, g, pt: (b, 0, new_blk))])
    return pl.pallas_call(
        functools.partial(_attn_sample_kernel, n_heads=n_heads, pages_per_step=pps),
        grid_spec=pltpu.PrefetchScalarGridSpec(
            num_scalar_prefetch=1,
            grid=(batch, n_pages // pps),
            in_specs=in_specs,
            out_specs=pl.BlockSpec((steps, d_attn), lambda b, g, pt: (b, 0)),
            scratch_shapes=[pltpu.VMEM((n_heads * steps, 1), jnp.float32),
                            pltpu.VMEM((n_heads * steps, 1), jnp.float32),
                            pltpu.VMEM((n_heads * steps, HEAD_DIM), jnp.float32)],
        ),
        out_shape=jax.ShapeDtypeStruct((m, d_attn), jnp.float32),
        compiler_params=_cparams(("arbitrary", "arbitrary")),
        name="attention_sample",
    )(page_table_flat, proj, proj, proj, *([k_pool] * pps), *([v_pool] * pps), c_rows, c_rows)


def _merge_kernel(ya_ref, yb_ref, wa_ref, wb_ref, ga_ref, gb_ref, z_ref):
    pa = jnp.dot(ya_ref[...].astype(jnp.bfloat16), wa_ref[...], preferred_element_type=jnp.float32)
    pb = jnp.dot(yb_ref[...].astype(jnp.bfloat16), wb_ref[...], preferred_element_type=jnp.float32)
    z = jax.nn.sigmoid(ga_ref[...]) * pa + jax.nn.sigmoid(gb_ref[...]) * pb
    z_ref[...] = z.astype(z_ref.dtype)


def branch_merge(ya, yb, w_branch, proj, col_gates):
    m, da = ya.shape
    db = yb.shape[1]
    d = w_branch.shape[2]
    tm = _tile(m, 512)
    tn = _tile(d, 1024)
    g0 = col_gates // tn
    g1 = (col_gates + d) // tn
    return pl.pallas_call(
        _merge_kernel,
        grid=(m // tm, d // tn),
        in_specs=[pl.BlockSpec((tm, da), lambda i, j: (i, 0)),
                  pl.BlockSpec((tm, db), lambda i, j: (i, 0)),
                  pl.BlockSpec((None, da, tn), lambda i, j: (0, 0, j)),
                  pl.BlockSpec((None, db, tn), lambda i, j: (1, 0, j)),
                  pl.BlockSpec((tm, tn), lambda i, j: (i, g0 + j)),
                  pl.BlockSpec((tm, tn), lambda i, j: (i, g1 + j))],
        out_specs=pl.BlockSpec((tm, tn), lambda i, j: (i, j)),
        out_shape=jax.ShapeDtypeStruct((m, d), jnp.bfloat16),
        compiler_params=_cparams(("arbitrary", "arbitrary")),
        name="branch_merge",
    )(ya, yb, w_branch, w_branch, proj, proj)


def _matmul_residual_kernel(a_ref, w_ref, x_ref, o_ref):
    o_ref[...] = x_ref[...] + jnp.dot(a_ref[...], w_ref[...], preferred_element_type=jnp.float32)


def matmul_residual(a, w, x):
    m, k = a.shape
    n = w.shape[1]
    pref = 1024 if k <= 2048 else 512
    tm = _tile(m, pref)
    tn = _tile(n, pref)
    return pl.pallas_call(
        _matmul_residual_kernel,
        grid=(m // tm, n // tn),
        in_specs=[pl.BlockSpec((tm, k), lambda i, j: (i, 0)),
                  pl.BlockSpec((k, tn), lambda i, j: (0, j)),
                  pl.BlockSpec((tm, tn), lambda i, j: (i, j))],
        out_specs=pl.BlockSpec((tm, tn), lambda i, j: (i, j)),
        out_shape=jax.ShapeDtypeStruct((m, n), jnp.float32),
        compiler_params=_cparams(("arbitrary", "arbitrary")),
        name="matmul_residual",
    )(a, w, x)


def _ffn_act(up_g, up_v, prev_g, prev_v, pos, cwg_ref, cbg_ref, cwv_ref, cbv_ref, width):
    gate = _causal_dwconv(up_g, prev_g, pos, cwg_ref, cbg_ref, width)
    val = _causal_dwconv(up_v, prev_v, pos, cwv_ref, cbv_ref, width)
    return jax.nn.gelu(gate) * val


def _ffn_up_prompt_kernel(xn_ref, wg_ref, wv_ref, cwg_ref, cbg_ref, cwv_ref, cbv_ref,
                          h_ref, tg_ref, tv_ref, pg_ref, pv_ref, *, width):
    @pl.when(pl.program_id(2) == 0)
    def _():
        pg_ref[...] = jnp.zeros_like(pg_ref)
        pv_ref[...] = jnp.zeros_like(pv_ref)

    xn = xn_ref[...]
    up_g = jnp.dot(xn, wg_ref[...], preferred_element_type=jnp.float32)
    up_v = jnp.dot(xn, wv_ref[...], preferred_element_type=jnp.float32)
    pos = lax.broadcasted_iota(jnp.int32, up_g.shape, 0)
    act = _ffn_act(up_g, up_v, pg_ref[...], pv_ref[...], pos, cwg_ref, cbg_ref, cwv_ref, cbv_ref, width)
    h_ref[...] = act.astype(h_ref.dtype)
    rows = up_g.shape[0]
    tg_ref[...] = up_g[rows - SUBLANES:, :]
    tv_ref[...] = up_v[rows - SUBLANES:, :]
    pg_ref[0:SUBLANES, :] = _tail_rows(up_g, width)
    pv_ref[0:SUBLANES, :] = _tail_rows(up_v, width)


def _ffn_up_sample_kernel(xn_ref, wg_ref, wv_ref, cwg_ref, cbg_ref, cwv_ref, cbv_ref, pg_ref, pv_ref,
                          h_ref, ug_ref, uv_ref, *, width, seq):
    xn = xn_ref[...]
    up_g = jnp.dot(xn, wg_ref[...], preferred_element_type=jnp.float32)
    up_v = jnp.dot(xn, wv_ref[...], preferred_element_type=jnp.float32)
    pos = lax.broadcasted_iota(jnp.int32, up_g.shape, 0) % seq
    act = _ffn_act(up_g, up_v, pg_ref[...], pv_ref[...], pos, cwg_ref, cbg_ref, cwv_ref, cbv_ref, width)
    h_ref[...] = act.astype(h_ref.dtype)
    ug_ref[...] = up_g
    uv_ref[...] = up_v


def ffn_up_prompt(xn, batch, seq, w_up, conv_w, conv_b):
    m, d = xn.shape
    f = w_up.shape[1] // 2
    width = conv_w.shape[0]
    tt = _tile(seq, 512)
    tn = _tile(f, 512)
    nt, nf = seq // tt, f // tn
    gcol = lambda n, b, t: (0, n)
    vcol = lambda n, b, t: (0, nf + n)
    h, tg, tv = pl.pallas_call(
        functools.partial(_ffn_up_prompt_kernel, width=width),
        grid=(nf, batch, nt),
        in_specs=[pl.BlockSpec((tt, d), lambda n, b, t: (b * nt + t, 0)),
                  pl.BlockSpec((d, tn), gcol),
                  pl.BlockSpec((d, tn), vcol),
                  pl.BlockSpec((width, tn), gcol),
                  pl.BlockSpec((1, tn), gcol),
                  pl.BlockSpec((width, tn), vcol),
                  pl.BlockSpec((1, tn), vcol)],
        out_specs=[pl.BlockSpec((tt, tn), lambda n, b, t: (b * nt + t, n)),
                   pl.BlockSpec((None, SUBLANES, tn), lambda n, b, t: (b, 0, n)),
                   pl.BlockSpec((None, SUBLANES, tn), lambda n, b, t: (b, 0, n))],
        out_shape=[jax.ShapeDtypeStruct((m, f), jnp.bfloat16),
                   jax.ShapeDtypeStruct((batch, SUBLANES, f), jnp.float32),
                   jax.ShapeDtypeStruct((batch, SUBLANES, f), jnp.float32)],
        scratch_shapes=[pltpu.VMEM((tt, tn), jnp.float32),
                        pltpu.VMEM((tt, tn), jnp.float32)],
        compiler_params=_cparams(("arbitrary", "arbitrary", "arbitrary")),
        name="ffn_up_prompt",
    )(xn, w_up, w_up, conv_w, conv_b, conv_w, conv_b)
    return h, tg, tv


def ffn_up_sample(xn, seq, w_up, conv_w, conv_b, prev_rows):
    m, d = xn.shape
    f = w_up.shape[1] // 2
    width = conv_w.shape[0]
    tt = _tile(m, 512)
    tn = _tile(f, 512)
    nf = f // tn
    gcol = lambda n, t: (0, n)
    vcol = lambda n, t: (0, nf + n)
    return pl.pallas_call(
        functools.partial(_ffn_up_sample_kernel, width=width, seq=seq),
        grid=(nf, m // tt),
        in_specs=[pl.BlockSpec((tt, d), lambda n, t: (t, 0)),
                  pl.BlockSpec((d, tn), gcol),
                  pl.BlockSpec((d, tn), vcol),
                  pl.BlockSpec((width, tn), gcol),
                  pl.BlockSpec((1, tn), gcol),
                  pl.BlockSpec((width, tn), vcol),
                  pl.BlockSpec((1, tn), vcol),
                  pl.BlockSpec((tt, tn), lambda n, t: (t, n)),
                  pl.BlockSpec((tt, tn), lambda n, t: (t, nf + n))],
        out_specs=[pl.BlockSpec((tt, tn), lambda n, t: (t, n)),
                   pl.BlockSpec((tt, tn), lambda n, t: (t, n)),
                   pl.BlockSpec((tt, tn), lambda n, t: (t, n))],
        out_shape=[jax.ShapeDtypeStruct((m, f), jnp.bfloat16),
                   jax.ShapeDtypeStruct((m, f), jnp.float32),
                   jax.ShapeDtypeStruct((m, f), jnp.float32)],
        compiler_params=_cparams(("arbitrary", "arbitrary")),
        name="ffn_up_sample",
    )(xn, w_up, w_up, conv_w, conv_b, conv_w, conv_b, prev_rows, prev_rows)


def _state_rows(state, seq):
    b, w, c = state.shape
    return jnp.pad(state, ((0, 0), (0, seq - w), (0, 0))).reshape(b * seq, c)


def kernel(x_prompt, x_sample, cache_k, cache_v, cache_logf, page_table, state_rnn_h, state_rnn_conv, state_ffn_conv, norm_mix_g, w_in, rnn_conv_w, rnn_conv_b, rg_w_a, rg_b_a, rg_w_x, rg_b_x, rg_lambda, fox_b_f, w_branch, w_out, norm_ffn_g, w_up, ffn_conv_w, ffn_conv_b, w_down, norm_final_g):
    bp, seq, d = x_prompt.shape
    bs, steps, _ = x_sample.shape
    depth = w_in.shape[0]
    d_rnn = rg_lambda.shape[1]
    n_heads = fox_b_f.shape[1]
    d_attn = n_heads * HEAD_DIM
    n_pages = page_table.shape[1]
    page = cache_k.shape[2]
    n_pool = cache_k.shape[1]
    f = w_down.shape[1]
    assert steps == SUBLANES and d_rnn == d_attn
    col_q = 2 * d_rnn
    col_f = col_q + 3 * d_attn
    col_gates = col_f + n_heads
    bf16 = jnp.bfloat16

    xp = x_prompt.reshape(bp * seq, d)
    xs = x_sample.reshape(bs * steps, d)
    pt_flat = page_table.reshape(-1)
    k_pool = cache_k.reshape(depth, n_pool, page * n_heads, HEAD_DIM)
    v_pool = cache_v.reshape(depth, n_pool, page * n_heads, HEAD_DIM)
    lf_pool_rows = jnp.swapaxes(cache_logf, 2, 3)

    outs = {name: [] for name in ("kp", "vp", "lfp", "hp", "rcp", "fcp", "ks", "vs", "lfs", "hs", "rcs", "fcs")}
    for l in range(depth):
        w_main = jnp.concatenate([w_in[l][:, :col_f], w_in[l][:, col_gates:]], axis=1).astype(bf16)
        w_f = jnp.pad(w_in[l][:, col_f:col_gates], ((0, 0), (0, LANES - n_heads))).astype(bf16)
        b_f = jnp.pad(fox_b_f[l], (0, LANES - n_heads)).reshape(1, LANES)
        wa = rg_w_a[l].astype(bf16)
        wx = rg_w_x[l].astype(bf16)
        vec = lambda a: a.reshape(1, -1)
        rg = (rnn_conv_w[l], vec(rnn_conv_b[l]), wa, wx, vec(rg_b_a[l]), vec(rg_b_x[l]), vec(rg_lambda[l]))
        wb = w_branch[l].astype(bf16)
        wo = w_out[l].astype(bf16)
        wu = w_up[l].astype(bf16)
        wd = w_down[l].astype(bf16)
        fcb = vec(ffn_conv_b[l])
        main_gates = col_f

        proj, logf = in_projection(xp, norm_mix_g[l], w_main, w_f, b_f)
        ya, h_last = rglru_prompt(proj, bp, seq, d_rnn, *rg)
        lf = logf[:, :n_heads]
        c_rows = cumsum_prompt(jnp.swapaxes(lf.reshape(bp, seq, n_heads), 1, 2))
        c_cols = jnp.swapaxes(c_rows, 1, 2).reshape(bp * seq, n_heads)
        yb = attention_prompt(proj, bp, seq, col_q, d_attn, c_cols, c_rows)
        z = branch_merge(ya, yb, wb, proj, main_gates)
        xp = matmul_residual(z, wo, xp)
        hid, tail_g, tail_v = ffn_up_prompt(rmsnorm(xp, norm_ffn_g[l], bf16), bp, seq, wu, ffn_conv_w[l], fcb)
        xp = matmul_residual(hid, wd, xp)
        proj3 = proj.reshape(bp, seq, -1)
        outs["kp"].append(proj3[:, :, col_q + d_attn:col_q + 2 * d_attn].reshape(bp, seq, n_heads, HEAD_DIM))
        outs["vp"].append(proj3[:, :, col_q + 2 * d_attn:col_q + 3 * d_attn].reshape(bp, seq, n_heads, HEAD_DIM))
        outs["lfp"].append(lf.reshape(bp, seq, n_heads))
        outs["hp"].append(h_last)
        outs["rcp"].append(proj3[:, seq - (rnn_conv_w.shape[1] - 1):, :d_rnn])
        fw = ffn_conv_w.shape[1] - 1
        outs["fcp"].append(jnp.concatenate([tail_g[:, SUBLANES - fw:], tail_v[:, SUBLANES - fw:]], axis=-1))

        proj, logf = in_projection(xs, norm_mix_g[l], w_main, w_f, b_f)
        ya, h_last = rglru_sample(proj, bs, steps, d_rnn, *rg,
                                  _state_rows(state_rnn_conv[l], steps),
                                  jnp.repeat(state_rnn_h[l], steps, axis=0))
        lf = logf[:, :n_heads]
        lf_new_rows = jnp.pad(jnp.swapaxes(lf.reshape(bs, steps, n_heads), 1, 2),
                              ((0, 0), (0, 0), (0, LANES - steps)))
        c_rows = cumsum_sample(pt_flat, n_pages, l, lf_pool_rows, lf_new_rows)
        yb = attention_sample(pt_flat, n_pages, l, proj, bs, steps, col_q, d_attn, k_pool, v_pool, c_rows)
        z = branch_merge(ya, yb, wb, proj, main_gates)
        xs = matmul_residual(z, wo, xs)
        hid, up_g, up_v = ffn_up_sample(rmsnorm(xs, norm_ffn_g[l], bf16), steps, wu, ffn_conv_w[l], fcb,
                                        _state_rows(state_ffn_conv[l], steps))
        xs = matmul_residual(hid, wd, xs)
        proj3 = proj.reshape(bs, steps, -1)
        outs["ks"].append(proj3[:, :, col_q + d_attn:col_q + 2 * d_attn].reshape(bs, steps, n_heads, HEAD_DIM))
        outs["vs"].append(proj3[:, :, col_q + 2 * d_attn:col_q + 3 * d_attn].reshape(bs, steps, n_heads, HEAD_DIM))
        outs["lfs"].append(lf.reshape(bs, steps, n_heads))
        outs["hs"].append(h_last)
        outs["rcs"].append(proj3[:, steps - (rnn_conv_w.shape[1] - 1):, :d_rnn])
        up = jnp.concatenate([up_g, up_v], axis=-1).reshape(bs, steps, 2 * f)
        outs["fcs"].append(up[:, steps - fw:])

    y_prompt = rmsnorm(xp, norm_final_g, jnp.float32).reshape(bp, seq, d)
    y_sample = rmsnorm(xs, norm_final_g, jnp.float32).reshape(bs, steps, d)
    st = lambda name: jnp.stack(outs[name])
    return (y_prompt, y_sample, st("kp"), st("vp"), st("lfp"), st("hp"), st("rcp"), st("fcp"),
            st("ks"), st("vs"), st("lfs"), st("hs"), st("rcs"), st("fcs"))
```

```python
import functools

import jax
import jax.numpy as jnp
from jax import lax
from jax.experimental import pallas as pl
from jax.experimental.pallas import tpu as pltpu

HEAD_DIM = 128
RNN_BLOCK_W = 128
RG_C = 8.0
NORM_EPS = 1e-6
SUBLANES = 8
LANES = 128
MASK_VALUE = -1e30
LOG2E = 1.4426950408889634
VMEM_LIMIT_BYTES = 56 * 1024 * 1024


def _cparams(semantics):
    return pltpu.CompilerParams(dimension_semantics=semantics, vmem_limit_bytes=VMEM_LIMIT_BYTES)


def _tile(dim, pref):
    t = min(dim, pref)
    assert dim % t == 0, (dim, pref)
    return t


def _nt_dot(a, b):
    return lax.dot_general(a, b, (((1,), (1,)), ((), ())), preferred_element_type=jnp.float32)


def _rmsnorm_rows(x, g):
    inv = lax.rsqrt(jnp.mean(x * x, axis=-1, keepdims=True) + NORM_EPS)
    return x * inv * g


def _rmsnorm_kernel(x_ref, g_ref, o_ref):
    o_ref[...] = _rmsnorm_rows(x_ref[...], g_ref[...]).astype(o_ref.dtype)


def rmsnorm(x, g, out_dtype):
    m, d = x.shape
    tm = _tile(m, 512)
    return pl.pallas_call(
        _rmsnorm_kernel,
        grid=(m // tm,),
        in_specs=[pl.BlockSpec((tm, d), lambda i: (i, 0)),
                  pl.BlockSpec((1, d), lambda i: (0, 0))],
        out_specs=pl.BlockSpec((tm, d), lambda i: (i, 0)),
        out_shape=jax.ShapeDtypeStruct((m, d), out_dtype),
        compiler_params=_cparams(("arbitrary",)),
        name="rmsnorm",
    )(x, g.reshape(1, d))


def _inproj_kernel(x_ref, g_ref, wa_ref, wb_ref, wf_ref, bf_ref, proj_ref, logf_ref, xn_ref, *, n_first):
    n = pl.program_id(1)

    @pl.when(n == 0)
    def _():
        xn = _rmsnorm_rows(x_ref[...], g_ref[...]).astype(jnp.bfloat16)
        xn_ref[...] = xn
        f = jnp.dot(xn, wf_ref[...], preferred_element_type=jnp.float32)
        logf_ref[...] = jax.nn.log_sigmoid(f + bf_ref[...])

    @pl.when(n < n_first)
    def _():
        proj_ref[...] = jnp.dot(xn_ref[...], wa_ref[...], preferred_element_type=jnp.float32)

    @pl.when(n >= n_first)
    def _():
        proj_ref[...] = jnp.dot(xn_ref[...], wb_ref[...], preferred_element_type=jnp.float32)


def in_projection(x, g, w_all, w_gates, w_f, b_f, layer, n_first_cols):
    m, d = x.shape
    n_gates = w_gates.shape[2]
    tm = _tile(m, 1024)
    tn = _tile(n_first_cols, 1024)
    assert n_gates % tn == 0
    n_first = n_first_cols // tn
    n_tiles = n_first + n_gates // tn
    return pl.pallas_call(
        functools.partial(_inproj_kernel, n_first=n_first),
        grid=(m // tm, n_tiles),
        in_specs=[pl.BlockSpec((tm, d), lambda i, j: (i, 0)),
                  pl.BlockSpec((1, d), lambda i, j: (0, 0)),
                  pl.BlockSpec((None, d, tn), lambda i, j: (layer, 0, jnp.minimum(j, n_first - 1))),
                  pl.BlockSpec((None, d, tn), lambda i, j: (layer, 0, jnp.maximum(j - n_first, 0))),
                  pl.BlockSpec((None, d, LANES), lambda i, j: (layer, 0, 0)),
                  pl.BlockSpec((1, LANES), lambda i, j: (0, 0))],
        out_specs=[pl.BlockSpec((tm, tn), lambda i, j: (i, j)),
                   pl.BlockSpec((tm, LANES), lambda i, j: (i, 0))],
        out_shape=[jax.ShapeDtypeStruct((m, n_first_cols + n_gates), jnp.float32),
                   jax.ShapeDtypeStruct((m, LANES), jnp.float32)],
        scratch_shapes=[pltpu.VMEM((tm, d), jnp.bfloat16)],
        compiler_params=_cparams(("arbitrary", "arbitrary")),
        name="in_projection",
    )(x, g.reshape(1, d), w_all, w_gates, w_f, b_f)


def _causal_dwconv(x, prev, pos, cw_ref, cb_ref, width):
    rows = x.shape[0]
    y = cb_ref[...] + cw_ref[width - 1:width, :] * x
    for d in range(1, width):
        up = width - 1 - d
        before = prev if up == 0 else pltpu.roll(prev, rows - up, axis=0)
        shifted = jnp.where(pos >= d, pltpu.roll(x, d, axis=0), before)
        y = y + cw_ref[width - 1 - d:width - d, :] * shifted
    return y


def _tail_rows(x, width):
    rows = x.shape[0]
    return pltpu.roll(x[rows - SUBLANES:, :], width - 1, axis=0)


def _rglru_core(x, xg, prev, pos, period, h_in, cw_ref, cb_ref, wa_ref, wx_ref, ba_ref, bx_ref, lam_ref, width):
    tc = x.shape[1]
    xc = _causal_dwconv(x, prev, pos, cw_ref, cb_ref, width)
    ga, gx = [], []
    for nb in range(tc // RNN_BLOCK_W):
        xb = xc[:, nb * RNN_BLOCK_W:(nb + 1) * RNN_BLOCK_W].astype(jnp.bfloat16)
        ga.append(jnp.dot(xb, wa_ref[nb], preferred_element_type=jnp.float32))
        gx.append(jnp.dot(xb, wx_ref[nb], preferred_element_type=jnp.float32))
    r_gate = jax.nn.sigmoid(jnp.concatenate(ga, axis=1) + ba_ref[...])
    i_gate = jax.nn.sigmoid(jnp.concatenate(gx, axis=1) + bx_ref[...])
    log_a = -RG_C * r_gate * jax.nn.softplus(-lam_ref[...])
    a = jnp.exp(log_a)
    u = jnp.sqrt(-jnp.tanh(log_a) * (a * a + 1.0)) * (i_gate * xc)
    s = 1
    while s < period:
        keep = pos >= s
        a_prev = jnp.where(keep, pltpu.roll(a, s, axis=0), 1.0)
        u_prev = jnp.where(keep, pltpu.roll(u, s, axis=0), 0.0)
        u = a * u_prev + u
        a = a * a_prev
        s *= 2
    h = a * h_in + u
    return h, h * jax.nn.gelu(xg)


def _rglru_prompt_kernel(xr_ref, xg_ref, cw_ref, cb_ref, wa_ref, wx_ref, ba_ref, bx_ref, lam_ref,
                         ya_ref, hl_ref, prev_ref, hc_ref, *, width):
    tt = xr_ref.shape[0]

    @pl.when(pl.program_id(2) == 0)
    def _():
        prev_ref[...] = jnp.zeros_like(prev_ref)
        hc_ref[...] = jnp.zeros_like(hc_ref)

    x = xr_ref[...]
    pos = lax.broadcasted_iota(jnp.int32, x.shape, 0)
    h_in = hc_ref[SUBLANES - 1:SUBLANES, :]
    h, ya = _rglru_core(x, xg_ref[...], prev_ref[...], pos, tt, h_in, cw_ref, cb_ref, wa_ref, wx_ref,
                        ba_ref, bx_ref, lam_ref, width)
    ya_ref[...] = ya.astype(ya_ref.dtype)
    hc_ref[...] = h[tt - SUBLANES:, :]
    hl_ref[...] = h[tt - 1:tt, :]
    prev_ref[0:SUBLANES, :] = _tail_rows(x, width)


def _rglru_sample_kernel(xr_ref, xg_ref, cw_ref, cb_ref, wa_ref, wx_ref, ba_ref, bx_ref, lam_ref,
                         prev_ref, h0_ref, ya_ref, hl_ref, hs_ref, *, width, seq):
    x = xr_ref[...]
    pos = lax.broadcasted_iota(jnp.int32, x.shape, 0) % seq
    h, ya = _rglru_core(x, xg_ref[...], prev_ref[...], pos, seq, h0_ref[...], cw_ref, cb_ref, wa_ref,
                        wx_ref, ba_ref, bx_ref, lam_ref, width)
    ya_ref[...] = ya.astype(ya_ref.dtype)
    for j in range(h.shape[1] // LANES):
        hs_ref[j] = h[:, j * LANES:(j + 1) * LANES]
        hl_ref[:, j * LANES:(j + 1) * LANES] = hs_ref[j, pl.ds(seq - 1, hl_ref.shape[0], stride=seq), :]


def rglru_prompt(proj, batch, seq, d_rnn, conv_w, conv_b, w_a, w_x, b_a, b_x, lam):
    m = proj.shape[0]
    width = conv_w.shape[0]
    tt = _tile(seq, 512)
    tc = _tile(d_rnn, 256)
    nt, nc = seq // tt, d_rnn // tc
    nb = tc // RNN_BLOCK_W
    row = lambda c, b, t: (b * nt + t, c)
    vec = lambda c, b, t: (0, c)
    ya, hl = pl.pallas_call(
        functools.partial(_rglru_prompt_kernel, width=width),
        grid=(nc, batch, nt),
        in_specs=[pl.BlockSpec((tt, tc), row),
                  pl.BlockSpec((tt, tc), lambda c, b, t: (b * nt + t, nc + c)),
                  pl.BlockSpec((width, tc), vec),
                  pl.BlockSpec((1, tc), vec),
                  pl.BlockSpec((nb, RNN_BLOCK_W, RNN_BLOCK_W), lambda c, b, t: (c, 0, 0)),
                  pl.BlockSpec((nb, RNN_BLOCK_W, RNN_BLOCK_W), lambda c, b, t: (c, 0, 0)),
                  pl.BlockSpec((1, tc), vec),
                  pl.BlockSpec((1, tc), vec),
                  pl.BlockSpec((1, tc), vec)],
        out_specs=[pl.BlockSpec((tt, tc), row),
                   pl.BlockSpec((None, 1, tc), lambda c, b, t: (b, 0, c))],
        out_shape=[jax.ShapeDtypeStruct((m, d_rnn), jnp.bfloat16),
                   jax.ShapeDtypeStruct((batch, 1, d_rnn), jnp.float32)],
        scratch_shapes=[pltpu.VMEM((tt, tc), jnp.float32),
                        pltpu.VMEM((SUBLANES, tc), jnp.float32)],
        compiler_params=_cparams(("arbitrary", "arbitrary", "arbitrary")),
        name="rglru_prompt",
    )(proj, proj, conv_w, conv_b, w_a, w_x, b_a, b_x, lam)
    return ya, hl.reshape(batch, d_rnn)


def rglru_sample(proj, batch, seq, d_rnn, conv_w, conv_b, w_a, w_x, b_a, b_x, lam, prev_rows, h0_rows):
    m = proj.shape[0]
    width = conv_w.shape[0]
    tt = _tile(m, 512)
    tc = _tile(d_rnn, 256)
    nc = d_rnn // tc
    nb = tc // RNN_BLOCK_W
    row = lambda c, t: (t, c)
    vec = lambda c, t: (0, c)
    ya, hl = pl.pallas_call(
        functools.partial(_rglru_sample_kernel, width=width, seq=seq),
        grid=(nc, m // tt),
        in_specs=[pl.BlockSpec((tt, tc), row),
                  pl.BlockSpec((tt, tc), lambda c, t: (t, nc + c)),
                  pl.BlockSpec((width, tc), vec),
                  pl.BlockSpec((1, tc), vec),
                  pl.BlockSpec((nb, RNN_BLOCK_W, RNN_BLOCK_W), lambda c, t: (c, 0, 0)),
                  pl.BlockSpec((nb, RNN_BLOCK_W, RNN_BLOCK_W), lambda c, t: (c, 0, 0)),
                  pl.BlockSpec((1, tc), vec),
                  pl.BlockSpec((1, tc), vec),
                  pl.BlockSpec((1, tc), vec),
                  pl.BlockSpec((tt, tc), row),
                  pl.BlockSpec((tt, tc), row)],
        out_specs=[pl.BlockSpec((tt, tc), row),
                   pl.BlockSpec((tt // seq, tc), row)],
        out_shape=[jax.ShapeDtypeStruct((m, d_rnn), jnp.bfloat16),
                   jax.ShapeDtypeStruct((batch, d_rnn), jnp.float32)],
        scratch_shapes=[pltpu.VMEM((tc // LANES, tt, LANES), jnp.float32)],
        compiler_params=_cparams(("arbitrary", "arbitrary")),
        name="rglru_sample",
    )(proj, proj, conv_w, conv_b, w_a, w_x, b_a, b_x, lam, prev_rows, h0_rows)
    return ya, hl


def _lane_cumsum(x):
    lane = lax.broadcasted_iota(jnp.int32, x.shape, 1)
    s = 1
    while s < x.shape[1]:
        x = x + jnp.where(lane >= s, pltpu.roll(x, s, axis=1), 0.0)
        s *= 2
    return x


def _cumsum_prompt_kernel(lf_ref, c_ref):
    c_ref[...] = _lane_cumsum(lf_ref[...])


def _cumsum_sample_kernel(pt_ref, *refs):
    del pt_ref
    c_ref = refs[-1]
    c_ref[...] = _lane_cumsum(jnp.concatenate([r[...] for r in refs[:-1]], axis=1))


def cumsum_prompt(lf_rows):
    b, h, t = lf_rows.shape
    return pl.pallas_call(
        _cumsum_prompt_kernel,
        grid=(b,),
        in_specs=[pl.BlockSpec((None, h, t), lambda i: (i, 0, 0))],
        out_specs=pl.BlockSpec((None, h, t), lambda i: (i, 0, 0)),
        out_shape=jax.ShapeDtypeStruct((b, h, t), jnp.float32),
        compiler_params=_cparams(("arbitrary",)),
        name="cumsum_prompt",
    )(lf_rows)


def cumsum_sample(page_table_flat, n_pages, layer, lf_pool_rows, lf_new_rows):
    b, h, tn = lf_new_rows.shape
    page = lf_pool_rows.shape[3]
    total = n_pages * page + tn

    def page_spec(i):
        return pl.BlockSpec((None, None, h, page), lambda s, pt: (layer, pt[s * n_pages + i], 0, 0))

    return pl.pallas_call(
        _cumsum_sample_kernel,
        grid_spec=pltpu.PrefetchScalarGridSpec(
            num_scalar_prefetch=1,
            grid=(b,),
            in_specs=[page_spec(i) for i in range(n_pages)]
            + [pl.BlockSpec((None, h, tn), lambda s, pt: (s, 0, 0))],
            out_specs=pl.BlockSpec((None, h, total), lambda s, pt: (s, 0, 0)),
        ),
        out_shape=jax.ShapeDtypeStruct((b, h, total), jnp.float32),
        compiler_params=_cparams(("arbitrary",)),
        name="cumsum_sample",
    )(page_table_flat, *([lf_pool_rows] * n_pages), lf_new_rows)


def _attn_prompt_kernel(q_ref, k_ref, v_ref, cq_ref, ck_ref, o_ref, q2_ref, cqw_ref, m_ref, l_ref, acc_ref,
                        *, n_heads):
    qi, ki = pl.program_id(1), pl.program_id(2)
    tq, tk = q_ref.shape[0], k_ref.shape[0]

    wide = (tq, LANES)

    @pl.when(ki == 0)
    def _():
        q2_ref[...] = (q_ref[...] * (HEAD_DIM ** -0.5 * LOG2E)).astype(jnp.bfloat16)
        cq2 = cq_ref[...] * LOG2E
        for h in range(n_heads):
            cqw_ref[h] = jnp.broadcast_to(cq2[:, h:h + 1], wide)
        m_ref[...] = jnp.full_like(m_ref, MASK_VALUE)
        l_ref[...] = jnp.zeros_like(l_ref)
        acc_ref[...] = jnp.zeros_like(acc_ref)

    def step(diagonal):
        k = k_ref[...].astype(jnp.bfloat16)
        v = v_ref[...].astype(jnp.bfloat16)
        ck2 = ck_ref[...] * LOG2E
        if diagonal:
            causal = (lax.broadcasted_iota(jnp.int32, (tq, tk), 1)
                      <= lax.broadcasted_iota(jnp.int32, (tq, tk), 0))
        for h in range(n_heads):
            hs = slice(h * HEAD_DIM, (h + 1) * HEAD_DIM)
            t = _nt_dot(q2_ref[:, hs], k[:, hs]) - ck2[h:h + 1, :]
            if diagonal:
                t = jnp.where(causal, t, MASK_VALUE)
            cq_h = cqw_ref[h]
            m_prev = m_ref[h]
            m_new = jnp.maximum(m_prev, jnp.broadcast_to(jnp.max(t, axis=-1, keepdims=True), wide) + cq_h)
            alpha = jnp.exp2(m_prev - m_new)
            shift = cq_h - m_new
            p = jnp.exp2(t + jnp.concatenate([shift] * (tk // LANES), axis=1))
            l_ref[h] = alpha * l_ref[h] + jnp.broadcast_to(jnp.sum(p, axis=-1, keepdims=True), wide)
            acc_ref[:, hs] = alpha * acc_ref[:, hs] + jnp.dot(p.astype(jnp.bfloat16), v[:, hs],
                                                              preferred_element_type=jnp.float32)
            m_ref[h] = m_new

    @pl.when(ki < qi)
    def _():
        step(False)

    @pl.when(ki == qi)
    def _():
        step(True)
        for h in range(n_heads):
            hs = slice(h * HEAD_DIM, (h + 1) * HEAD_DIM)
            o_ref[:, hs] = (acc_ref[:, hs] / l_ref[h]).astype(o_ref.dtype)


def attention_prompt(proj, batch, seq, col_q, d_attn, c_cols, c_rows):
    m = proj.shape[0]
    n_heads = d_attn // HEAD_DIM
    tq = _tile(seq, 512)
    nq = seq // tq
    cb = col_q // d_attn
    kv_row = lambda b, qi, ki: b * nq + jnp.minimum(ki, qi)
    return pl.pallas_call(
        functools.partial(_attn_prompt_kernel, n_heads=n_heads),
        grid=(batch, nq, nq),
        in_specs=[pl.BlockSpec((tq, d_attn), lambda b, qi, ki: (b * nq + qi, cb)),
                  pl.BlockSpec((tq, d_attn), lambda b, qi, ki: (kv_row(b, qi, ki), cb + 1)),
                  pl.BlockSpec((tq, d_attn), lambda b, qi, ki: (kv_row(b, qi, ki), cb + 2)),
                  pl.BlockSpec((tq, n_heads), lambda b, qi, ki: (b * nq + qi, 0)),
                  pl.BlockSpec((None, n_heads, tq), lambda b, qi, ki: (b, 0, jnp.minimum(ki, qi)))],
        out_specs=pl.BlockSpec((tq, d_attn), lambda b, qi, ki: (b * nq + qi, 0)),
        out_shape=jax.ShapeDtypeStruct((m, d_attn), jnp.bfloat16),
        scratch_shapes=[pltpu.VMEM((tq, d_attn), jnp.bfloat16),
                        pltpu.VMEM((n_heads, tq, LANES), jnp.float32),
                        pltpu.VMEM((n_heads, tq, LANES), jnp.float32),
                        pltpu.VMEM((n_heads, tq, LANES), jnp.float32),
                        pltpu.VMEM((tq, d_attn), jnp.float32)],
        compiler_params=_cparams(("arbitrary", "arbitrary", "arbitrary")),
        name="attention_prompt",
    )(proj, proj, proj, c_cols, c_rows)


def _attn_sample_kernel(pt_ref, *refs, n_heads, pages_per_step):
    del pt_ref
    pps = pages_per_step
    q_ref, kn_ref, vn_ref = refs[0:3]
    k_refs = refs[3:3 + pps]
    v_refs = refs[3 + pps:3 + 2 * pps]
    ck_ref, cn_ref, o_ref, m_ref, l_ref, acc_ref = refs[3 + 2 * pps:]
    g = pl.program_id(1)
    steps = q_ref.shape[0]
    page = k_refs[0].shape[0] // n_heads
    scale = HEAD_DIM ** -0.5

    @pl.when(g == 0)
    def _():
        m_ref[...] = jnp.full_like(m_ref, MASK_VALUE)
        l_ref[...] = jnp.zeros_like(l_ref)
        acc_ref[...] = jnp.zeros_like(acc_ref)

    q = q_ref[...].astype(jnp.bfloat16)
    cn = cn_ref[...]
    ck = ck_ref[...]
    sub = lax.broadcasted_iota(jnp.int32, (steps, LANES), 0)
    lane = lax.broadcasted_iota(jnp.int32, (steps, LANES), 1)
    cq = jnp.concatenate(
        [jnp.sum(jnp.where(sub == lane, cn[h:h + 1, :], 0.0), axis=1, keepdims=True) for h in range(n_heads)],
        axis=0)

    def rows_of(c):
        return jnp.concatenate([jnp.broadcast_to(c[h:h + 1, :], (steps, c.shape[1])) for h in range(n_heads)],
                               axis=0)

    def head_of(page_refs, h):
        return jnp.concatenate([r[pl.ds(h, page, stride=n_heads), :] for r in page_refs],
                               axis=0).astype(jnp.bfloat16)

    def update(s, v_of):
        m_prev = m_ref[...]
        m_new = jnp.maximum(m_prev, jnp.max(s, axis=-1, keepdims=True))
        alpha = jnp.exp(m_prev - m_new)
        p = jnp.exp(s - m_new)
        l_ref[...] = alpha * l_ref[...] + jnp.sum(p, axis=-1, keepdims=True)
        m_ref[...] = m_new
        pv = [jnp.dot(p[h * steps:(h + 1) * steps, :].astype(jnp.bfloat16), v_of(h),
                      preferred_element_type=jnp.float32) for h in range(n_heads)]
        acc_ref[...] = alpha * acc_ref[...] + jnp.concatenate(pv, axis=0)

    s = jnp.concatenate(
        [_nt_dot(q[:, h * HEAD_DIM:(h + 1) * HEAD_DIM], head_of(k_refs, h)) for h in range(n_heads)], axis=0)
    update(s * scale + cq - rows_of(ck), lambda h: head_of(v_refs, h))

    @pl.when(g == pl.num_programs(1) - 1)
    def _():
        pad = jnp.zeros((LANES - steps, HEAD_DIM), jnp.bfloat16)

        def new_of(ref, h):
            return jnp.concatenate([ref[:, h * HEAD_DIM:(h + 1) * HEAD_DIM].astype(jnp.bfloat16), pad], axis=0)

        s_new = jnp.concatenate(
            [_nt_dot(q[:, h * HEAD_DIM:(h + 1) * HEAD_DIM], new_of(kn_ref, h)) for h in range(n_heads)], axis=0)
        causal = jnp.concatenate([lane <= sub] * n_heads, axis=0)
        update(jnp.where(causal, s_new * scale + cq - rows_of(cn), MASK_VALUE), lambda h: new_of(vn_ref, h))
        out = acc_ref[...] / l_ref[...]
        for h in range(n_heads):
            o_ref[:, h * HEAD_DIM:(h + 1) * HEAD_DIM] = out[h * steps:(h + 1) * steps, :].astype(o_ref.dtype)


def attention_sample(page_table_flat, n_pages, layer, proj, batch, steps, col_q, d_attn, k_pool, v_pool, c_rows):
    m = proj.shape[0]
    n_heads = d_attn // HEAD_DIM
    page = k_pool.shape[2] // n_heads
    pps = _tile(n_pages, 16)
    cb = col_q // d_attn

    def page_spec(i):
        return pl.BlockSpec((None, None, page * n_heads, HEAD_DIM),
                            lambda b, g, pt: (layer, pt[b * n_pages + g * pps + i], 0, 0))

    new_blk = n_pages * page // LANES
    in_specs = ([pl.BlockSpec((steps, d_attn), lambda b, g, pt: (b, cb)),
                 pl.BlockSpec((steps, d_attn), lambda b, g, pt: (b, cb + 1)),
                 pl.BlockSpec((steps, d_attn), lambda b, g, pt: (b, cb + 2))]
                + [page_spec(i) for i in range(pps)]
                + [page_spec(i) for i in range(pps)]
                + [pl.BlockSpec((None, n_heads, pps * page), lambda b, g, pt: (b, 0, g)),
                   pl.BlockSpec((None, n_heads, LANES), lambda b, g, pt: (b, 0, new_blk))])
    return pl.pallas_call(
        functools.partial(_attn_sample_kernel, n_heads=n_heads, pages_per_step=pps),
        grid_spec=pltpu.PrefetchScalarGridSpec(
            num_scalar_prefetch=1,
            grid=(batch, n_pages // pps),
            in_specs=in_specs,
            out_specs=pl.BlockSpec((steps, d_attn), lambda b, g, pt: (b, 0)),
            scratch_shapes=[pltpu.VMEM((n_heads * steps, 1), jnp.float32),
                            pltpu.VMEM((n_heads * steps, 1), jnp.float32),
                            pltpu.VMEM((n_heads * steps, HEAD_DIM), jnp.float32)],
        ),
        out_shape=jax.ShapeDtypeStruct((m, d_attn), jnp.float32),
        compiler_params=_cparams(("arbitrary", "arbitrary")),
        name="attention_sample",
    )(page_table_flat, proj, proj, proj, *([k_pool] * pps), *([v_pool] * pps), c_rows, c_rows)


def _merge_kernel(ya_ref, yb_ref, wa_ref, wb_ref, ga_ref, gb_ref, z_ref, wa16_ref, wb16_ref):
    @pl.when(pl.program_id(1) == 0)
    def _():
        wa16_ref[...] = wa_ref[...].astype(jnp.bfloat16)
        wb16_ref[...] = wb_ref[...].astype(jnp.bfloat16)

    pa = jnp.dot(ya_ref[...].astype(jnp.bfloat16), wa16_ref[...], preferred_element_type=jnp.float32)
    pb = jnp.dot(yb_ref[...].astype(jnp.bfloat16), wb16_ref[...], preferred_element_type=jnp.float32)
    z = jax.nn.sigmoid(ga_ref[...]) * pa + jax.nn.sigmoid(gb_ref[...]) * pb
    z_ref[...] = z.astype(z_ref.dtype)


def branch_merge(ya, yb, w_branch, layer, proj, col_gates):
    m, da = ya.shape
    db = yb.shape[1]
    d = w_branch.shape[3]
    tm = _tile(m, 1024)
    tn = _tile(d, 512)
    g0 = col_gates // tn
    g1 = (col_gates + d) // tn
    return pl.pallas_call(
        _merge_kernel,
        grid=(d // tn, m // tm),
        in_specs=[pl.BlockSpec((tm, da), lambda j, i: (i, 0)),
                  pl.BlockSpec((tm, db), lambda j, i: (i, 0)),
                  pl.BlockSpec((None, None, da, tn), lambda j, i: (layer, 0, 0, j)),
                  pl.BlockSpec((None, None, db, tn), lambda j, i: (layer, 1, 0, j)),
                  pl.BlockSpec((tm, tn), lambda j, i: (i, g0 + j)),
                  pl.BlockSpec((tm, tn), lambda j, i: (i, g1 + j))],
        out_specs=pl.BlockSpec((tm, tn), lambda j, i: (i, j)),
        out_shape=jax.ShapeDtypeStruct((m, d), jnp.bfloat16),
        scratch_shapes=[pltpu.VMEM((da, tn), jnp.bfloat16),
                        pltpu.VMEM((db, tn), jnp.bfloat16)],
        compiler_params=_cparams(("arbitrary", "arbitrary")),
        name="branch_merge",
    )(ya, yb, w_branch, w_branch, proj, proj)


def _matmul_residual_kernel(a_ref, w_ref, x_ref, o_ref, w16_ref):
    @pl.when(pl.program_id(1) == 0)
    def _():
        w16_ref[...] = w_ref[...].astype(jnp.bfloat16)

    o_ref[...] = x_ref[...] + jnp.dot(a_ref[...], w16_ref[...], preferred_element_type=jnp.float32)


def matmul_residual(a, w, layer, x):
    m, k = a.shape
    n = w.shape[2]
    tm = _tile(m, 1024)
    tn = _tile(n, 512 if k <= 2048 else 256)
    return pl.pallas_call(
        _matmul_residual_kernel,
        grid=(n // tn, m // tm),
        in_specs=[pl.BlockSpec((tm, k), lambda j, i: (i, 0)),
                  pl.BlockSpec((None, k, tn), lambda j, i: (layer, 0, j)),
                  pl.BlockSpec((tm, tn), lambda j, i: (i, j))],
        out_specs=pl.BlockSpec((tm, tn), lambda j, i: (i, j)),
        out_shape=jax.ShapeDtypeStruct((m, n), jnp.float32),
        scratch_shapes=[pltpu.VMEM((k, tn), jnp.bfloat16)],
        compiler_params=_cparams(("arbitrary", "arbitrary")),
        name="matmul_residual",
    )(a, w, x)


def _ffn_act(up_g, up_v, prev_g, prev_v, pos, cwg_ref, cbg_ref, cwv_ref, cbv_ref, width):
    gate = _causal_dwconv(up_g, prev_g, pos, cwg_ref, cbg_ref, width)
    val = _causal_dwconv(up_v, prev_v, pos, cwv_ref, cbv_ref, width)
    return jax.nn.gelu(gate) * val


def _ffn_up_prompt_kernel(xn_ref, wg_ref, wv_ref, cwg_ref, cbg_ref, cwv_ref, cbv_ref,
                          h_ref, tg_ref, tv_ref, pg_ref, pv_ref, wg16_ref, wv16_ref, *, width):
    @pl.when((pl.program_id(1) == 0) & (pl.program_id(2) == 0))
    def _():
        wg16_ref[...] = wg_ref[...].astype(jnp.bfloat16)
        wv16_ref[...] = wv_ref[...].astype(jnp.bfloat16)

    @pl.when(pl.program_id(2) == 0)
    def _():
        pg_ref[...] = jnp.zeros_like(pg_ref)
        pv_ref[...] = jnp.zeros_like(pv_ref)

    xn = xn_ref[...]
    up_g = jnp.dot(xn, wg16_ref[...], preferred_element_type=jnp.float32)
    up_v = jnp.dot(xn, wv16_ref[...], preferred_element_type=jnp.float32)
    pos = lax.broadcasted_iota(jnp.int32, up_g.shape, 0)
    act = _ffn_act(up_g, up_v, pg_ref[...], pv_ref[...], pos, cwg_ref, cbg_ref, cwv_ref, cbv_ref, width)
    h_ref[...] = act.astype(h_ref.dtype)
    rows = up_g.shape[0]
    tg_ref[...] = up_g[rows - SUBLANES:, :]
    tv_ref[...] = up_v[rows - SUBLANES:, :]
    pg_ref[0:SUBLANES, :] = _tail_rows(up_g, width)
    pv_ref[0:SUBLANES, :] = _tail_rows(up_v, width)


def _ffn_up_sample_kernel(xn_ref, wg_ref, wv_ref, cwg_ref, cbg_ref, cwv_ref, cbv_ref, pg_ref, pv_ref,
                          h_ref, ug_ref, uv_ref, wg16_ref, wv16_ref, *, width, seq):
    @pl.when(pl.program_id(1) == 0)
    def _():
        wg16_ref[...] = wg_ref[...].astype(jnp.bfloat16)
        wv16_ref[...] = wv_ref[...].astype(jnp.bfloat16)

    xn = xn_ref[...]
    up_g = jnp.dot(xn, wg16_ref[...], preferred_element_type=jnp.float32)
    up_v = jnp.dot(xn, wv16_ref[...], preferred_element_type=jnp.float32)
    pos = lax.broadcasted_iota(jnp.int32, up_g.shape, 0) % seq
    act = _ffn_act(up_g, up_v, pg_ref[...], pv_ref[...], pos, cwg_ref, cbg_ref, cwv_ref, cbv_ref, width)
    h_ref[...] = act.astype(h_ref.dtype)
    ug_ref[...] = up_g
    uv_ref[...] = up_v


def ffn_up_prompt(xn, batch, seq, w_up, layer, conv_w, conv_b):
    m, d = xn.shape
    f = w_up.shape[2] // 2
    width = conv_w.shape[0]
    tt = _tile(seq, 512)
    tn = _tile(f, 512)
    nt, nf = seq // tt, f // tn
    gcol = lambda n, b, t: (0, n)
    vcol = lambda n, b, t: (0, nf + n)
    h, tg, tv = pl.pallas_call(
        functools.partial(_ffn_up_prompt_kernel, width=width),
        grid=(nf, batch, nt),
        in_specs=[pl.BlockSpec((tt, d), lambda n, b, t: (b * nt + t, 0)),
                  pl.BlockSpec((None, d, tn), lambda n, b, t: (layer, 0, n)),
                  pl.BlockSpec((None, d, tn), lambda n, b, t: (layer, 0, nf + n)),
                  pl.BlockSpec((width, tn), gcol),
                  pl.BlockSpec((1, tn), gcol),
                  pl.BlockSpec((width, tn), vcol),
                  pl.BlockSpec((1, tn), vcol)],
        out_specs=[pl.BlockSpec((tt, tn), lambda n, b, t: (b * nt + t, n)),
                   pl.BlockSpec((None, SUBLANES, tn), lambda n, b, t: (b, 0, n)),
                   pl.BlockSpec((None, SUBLANES, tn), lambda n, b, t: (b, 0, n))],
        out_shape=[jax.ShapeDtypeStruct((m, f), jnp.bfloat16),
                   jax.ShapeDtypeStruct((batch, SUBLANES, f), jnp.float32),
                   jax.ShapeDtypeStruct((batch, SUBLANES, f), jnp.float32)],
        scratch_shapes=[pltpu.VMEM((tt, tn), jnp.float32),
                        pltpu.VMEM((tt, tn), jnp.float32),
                        pltpu.VMEM((d, tn), jnp.bfloat16),
                        pltpu.VMEM((d, tn), jnp.bfloat16)],
        compiler_params=_cparams(("arbitrary", "arbitrary", "arbitrary")),
        name="ffn_up_prompt",
    )(xn, w_up, w_up, conv_w, conv_b, conv_w, conv_b)
    return h, tg, tv


def ffn_up_sample(xn, seq, w_up, layer, conv_w, conv_b, prev_rows):
    m, d = xn.shape
    f = w_up.shape[2] // 2
    width = conv_w.shape[0]
    tt = _tile(m, 512)
    tn = _tile(f, 512)
    nf = f // tn
    gcol = lambda n, t: (0, n)
    vcol = lambda n, t: (0, nf + n)
    return pl.pallas_call(
        functools.partial(_ffn_up_sample_kernel, width=width, seq=seq),
        grid=(nf, m // tt),
        in_specs=[pl.BlockSpec((tt, d), lambda n, t: (t, 0)),
                  pl.BlockSpec((None, d, tn), lambda n, t: (layer, 0, n)),
                  pl.BlockSpec((None, d, tn), lambda n, t: (layer, 0, nf + n)),
                  pl.BlockSpec((width, tn), gcol),
                  pl.BlockSpec((1, tn), gcol),
                  pl.BlockSpec((width, tn), vcol),
                  pl.BlockSpec((1, tn), vcol),
                  pl.BlockSpec((tt, tn), lambda n, t: (t, n)),
                  pl.BlockSpec((tt, tn), lambda n, t: (t, nf + n))],
        out_specs=[pl.BlockSpec((tt, tn), lambda n, t: (t, n)),
                   pl.BlockSpec((tt, tn), lambda n, t: (t, n)),
                   pl.BlockSpec((tt, tn), lambda n, t: (t, n))],
        out_shape=[jax.ShapeDtypeStruct((m, f), jnp.bfloat16),
                   jax.ShapeDtypeStruct((m, f), jnp.float32),
                   jax.ShapeDtypeStruct((m, f), jnp.float32)],
        scratch_shapes=[pltpu.VMEM((d, tn), jnp.bfloat16),
                        pltpu.VMEM((d, tn), jnp.bfloat16)],
        compiler_params=_cparams(("arbitrary", "arbitrary")),
        name="ffn_up_sample",
    )(xn, w_up, w_up, conv_w, conv_b, conv_w, conv_b, prev_rows, prev_rows)


def _state_rows(state, seq):
    b, w, c = state.shape
    return jnp.pad(state, ((0, 0), (0, seq - w), (0, 0))).reshape(b * seq, c)


def kernel(x_prompt, x_sample, cache_k, cache_v, cache_logf, page_table, state_rnn_h, state_rnn_conv, state_ffn_conv, norm_mix_g, w_in, rnn_conv_w, rnn_conv_b, rg_w_a, rg_b_a, rg_w_x, rg_b_x, rg_lambda, fox_b_f, w_branch, w_out, norm_ffn_g, w_up, ffn_conv_w, ffn_conv_b, w_down, norm_final_g):
    bp, seq, d = x_prompt.shape
    bs, steps, _ = x_sample.shape
    depth = w_in.shape[0]
    d_rnn = rg_lambda.shape[1]
    n_heads = fox_b_f.shape[1]
    d_attn = n_heads * HEAD_DIM
    n_pages = page_table.shape[1]
    page = cache_k.shape[2]
    n_pool = cache_k.shape[1]
    f = w_down.shape[1]
    assert steps == SUBLANES and d_rnn == d_attn
    col_q = 2 * d_rnn
    col_f = col_q + 3 * d_attn
    col_gates = col_f + n_heads
    bf16 = jnp.bfloat16

    xp = x_prompt.reshape(bp * seq, d)
    xs = x_sample.reshape(bs * steps, d)
    pt_flat = page_table.reshape(-1)
    k_pool = cache_k.reshape(depth, n_pool, page * n_heads, HEAD_DIM)
    v_pool = cache_v.reshape(depth, n_pool, page * n_heads, HEAD_DIM)
    lf_pool_rows = jnp.swapaxes(cache_logf, 2, 3)

    outs = {name: [] for name in ("kp", "vp", "lfp", "hp", "rcp", "fcp", "ks", "vs", "lfs", "hs", "rcs", "fcs")}
    w_in16 = w_in.astype(bf16)
    w_gates16 = w_in[:, :, col_gates:].astype(bf16)
    w_f16 = jnp.pad(w_in[:, :, col_f:col_gates], ((0, 0), (0, 0), (0, LANES - n_heads))).astype(bf16)
    rg_wa16 = rg_w_a.astype(bf16)
    rg_wx16 = rg_w_x.astype(bf16)
    fw = ffn_conv_w.shape[1] - 1
    for l in range(depth):
        b_f = jnp.pad(fox_b_f[l], (0, LANES - n_heads)).reshape(1, LANES)
        vec = lambda a: a.reshape(1, -1)
        rg = (rnn_conv_w[l], vec(rnn_conv_b[l]), rg_wa16[l], rg_wx16[l], vec(rg_b_a[l]), vec(rg_b_x[l]),
              vec(rg_lambda[l]))
        fcb = vec(ffn_conv_b[l])
        main_gates = col_f
        inproj_w = (w_in16, w_gates16, w_f16, b_f, l, col_f)

        proj, logf = in_projection(xp, norm_mix_g[l], *inproj_w)
        ya, h_last = rglru_prompt(proj, bp, seq, d_rnn, *rg)
        lf = logf[:, :n_heads]
        c_rows = cumsum_prompt(jnp.swapaxes(lf.reshape(bp, seq, n_heads), 1, 2))
        c_cols = jnp.swapaxes(c_rows, 1, 2).reshape(bp * seq, n_heads)
        yb = attention_prompt(proj, bp, seq, col_q, d_attn, c_cols, c_rows)
        z = branch_merge(ya, yb, w_branch, l, proj, main_gates)
        xp = matmul_residual(z, w_out, l, xp)
        hid, tail_g, tail_v = ffn_up_prompt(rmsnorm(xp, norm_ffn_g[l], bf16), bp, seq, w_up, l,
                                            ffn_conv_w[l], fcb)
        xp = matmul_residual(hid, w_down, l, xp)
        proj3 = proj.reshape(bp, seq, -1)
        outs["kp"].append(proj3[:, :, col_q + d_attn:col_q + 2 * d_attn].reshape(bp, seq, n_heads, HEAD_DIM))
        outs["vp"].append(proj3[:, :, col_q + 2 * d_attn:col_q + 3 * d_attn].reshape(bp, seq, n_heads, HEAD_DIM))
        outs["lfp"].append(lf.reshape(bp, seq, n_heads))
        outs["hp"].append(h_last)
        outs["rcp"].append(proj3[:, seq - (rnn_conv_w.shape[1] - 1):, :d_rnn])
        outs["fcp"].append(jnp.concatenate([tail_g[:, SUBLANES - fw:], tail_v[:, SUBLANES - fw:]], axis=-1))

        proj, logf = in_projection(xs, norm_mix_g[l], *inproj_w)
        ya, h_last = rglru_sample(proj, bs, steps, d_rnn, *rg,
                                  _state_rows(state_rnn_conv[l], steps),
                                  jnp.repeat(state_rnn_h[l], steps, axis=0))
        lf = logf[:, :n_heads]
        lf_new_rows = jnp.pad(jnp.swapaxes(lf.reshape(bs, steps, n_heads), 1, 2),
                              ((0, 0), (0, 0), (0, LANES - steps)))
        c_rows = cumsum_sample(pt_flat, n_pages, l, lf_pool_rows, lf_new_rows)
        yb = attention_sample(pt_flat, n_pages, l, proj, bs, steps, col_q, d_attn, k_pool, v_pool, c_rows)
        z = branch_merge(ya, yb, w_branch, l, proj, main_gates)
        xs = matmul_residual(z, w_out, l, xs)
        hid, up_g, up_v = ffn_up_sample(rmsnorm(xs, norm_ffn_g[l], bf16), steps, w_up, l, ffn_conv_w[l], fcb,
                                        _state_rows(state_ffn_conv[l], steps))
        xs = matmul_residual(hid, w_down, l, xs)
        proj3 = proj.reshape(bs, steps, -1)
        outs["ks"].append(proj3[:, :, col_q + d_attn:col_q + 2 * d_attn].reshape(bs, steps, n_heads, HEAD_DIM))
        outs["vs"].append(proj3[:, :, col_q + 2 * d_attn:col_q + 3 * d_attn].reshape(bs, steps, n_heads, HEAD_DIM))
        outs["lfs"].append(lf.reshape(bs, steps, n_heads))
        outs["hs"].append(h_last)
        outs["rcs"].append(proj3[:, steps - (rnn_conv_w.shape[1] - 1):, :d_rnn])
        outs["fcs"].append(jnp.concatenate([up_g.reshape(bs, steps, f)[:, steps - fw:],
                                            up_v.reshape(bs, steps, f)[:, steps - fw:]], axis=-1))

    y_prompt = rmsnorm(xp, norm_final_g, jnp.float32).reshape(bp, seq, d)
    y_sample = rmsnorm(xs, norm_final_g, jnp.float32).reshape(bs, steps, d)
    st = lambda name: jnp.stack(outs[name])
    return (y_prompt, y_sample, st("kp"), st("vp"), st("lfp"), st("hp"), st("rcp"), st("fcp"),
            st("ks"), st("vs"), st("lfs"), st("hs"), st("rcs"), st("fcs"))
```

```python
import functools

import jax
import jax.numpy as jnp
from jax import lax
from jax.experimental import pallas as pl
from jax.experimental.pallas import tpu as pltpu

HEAD_DIM = 128
RNN_BLOCK_W = 128
RG_C = 8.0
NORM_EPS = 1e-6
SUBLANES = 8
LANES = 128
MASK_VALUE = -1e30
LOG2E = 1.4426950408889634
VMEM_LIMIT_BYTES = 56 * 1024 * 1024


def _cparams(semantics):
    return pltpu.CompilerParams(dimension_semantics=semantics, vmem_limit_bytes=VMEM_LIMIT_BYTES)


def _tile(dim, pref):
    t = min(dim, pref)
    assert dim % t == 0, (dim, pref)
    return t


def _nt_dot(a, b):
    return lax.dot_general(a, b, (((1,), (1,)), ((), ())), preferred_element_type=jnp.float32)


def _rmsnorm_rows(x, g):
    inv = lax.rsqrt(jnp.mean(x * x, axis=-1, keepdims=True) + NORM_EPS)
    return x * inv * g


def _rmsnorm_kernel(x_ref, g_ref, o_ref):
    o_ref[...] = _rmsnorm_rows(x_ref[...], g_ref[...]).astype(o_ref.dtype)


def rmsnorm(x, g, out_dtype):
    m, d = x.shape
    tm = _tile(m, 512)
    return pl.pallas_call(
        _rmsnorm_kernel,
        grid=(m // tm,),
        in_specs=[pl.BlockSpec((tm, d), lambda i: (i, 0)),
                  pl.BlockSpec((1, d), lambda i: (0, 0))],
        out_specs=pl.BlockSpec((tm, d), lambda i: (i, 0)),
        out_shape=jax.ShapeDtypeStruct((m, d), out_dtype),
        compiler_params=_cparams(("arbitrary",)),
        name="rmsnorm",
    )(x, g.reshape(1, d))


def _inproj_kernel(x_ref, g_ref, wa_ref, wb_ref, wf_ref, bf_ref, proj_ref, logf_ref, xn_ref, *, n_first):
    n = pl.program_id(1)

    @pl.when(n == 0)
    def _():
        xn = _rmsnorm_rows(x_ref[...], g_ref[...]).astype(jnp.bfloat16)
        xn_ref[...] = xn
        f = jnp.dot(xn, wf_ref[...], preferred_element_type=jnp.float32)
        logf_ref[...] = jax.nn.log_sigmoid(f + bf_ref[...])

    @pl.when(n < n_first)
    def _():
        proj_ref[...] = jnp.dot(xn_ref[...], wa_ref[...], preferred_element_type=jnp.float32)

    @pl.when(n >= n_first)
    def _():
        proj_ref[...] = jnp.dot(xn_ref[...], wb_ref[...], preferred_element_type=jnp.float32)


def in_projection(x, g, w_all, w_gates, w_f, b_f, layer, n_first_cols):
    m, d = x.shape
    n_gates = w_gates.shape[2]
    tm = _tile(m, 1024)
    tn = _tile(n_first_cols, 1024)
    assert n_gates % tn == 0
    n_first = n_first_cols // tn
    n_tiles = n_first + n_gates // tn
    return pl.pallas_call(
        functools.partial(_inproj_kernel, n_first=n_first),
        grid=(m // tm, n_tiles),
        in_specs=[pl.BlockSpec((tm, d), lambda i, j: (i, 0)),
                  pl.BlockSpec((1, d), lambda i, j: (0, 0)),
                  pl.BlockSpec((None, d, tn), lambda i, j: (layer, 0, jnp.minimum(j, n_first - 1))),
                  pl.BlockSpec((None, d, tn), lambda i, j: (layer, 0, jnp.maximum(j - n_first, 0))),
                  pl.BlockSpec((None, d, LANES), lambda i, j: (layer, 0, 0)),
                  pl.BlockSpec((1, LANES), lambda i, j: (0, 0))],
        out_specs=[pl.BlockSpec((tm, tn), lambda i, j: (i, j)),
                   pl.BlockSpec((tm, LANES), lambda i, j: (i, 0))],
        out_shape=[jax.ShapeDtypeStruct((m, n_first_cols + n_gates), jnp.float32),
                   jax.ShapeDtypeStruct((m, LANES), jnp.float32)],
        scratch_shapes=[pltpu.VMEM((tm, d), jnp.bfloat16)],
        compiler_params=_cparams(("arbitrary", "arbitrary")),
        name="in_projection",
    )(x, g.reshape(1, d), w_all, w_gates, w_f, b_f)


def _causal_dwconv(x, prev, pos, cw_ref, cb_ref, width):
    rows = x.shape[0]
    y = cb_ref[...] + cw_ref[width - 1:width, :] * x
    for d in range(1, width):
        up = width - 1 - d
        before = prev if up == 0 else pltpu.roll(prev, rows - up, axis=0)
        shifted = jnp.where(pos >= d, pltpu.roll(x, d, axis=0), before)
        y = y + cw_ref[width - 1 - d:width - d, :] * shifted
    return y


def _tail_rows(x, width):
    rows = x.shape[0]
    return pltpu.roll(x[rows - SUBLANES:, :], width - 1, axis=0)


def _rglru_core(x, xg, prev, pos, period, h_in, cw_ref, cb_ref, wa_ref, wx_ref, ba_ref, bx_ref, lam_ref, width):
    tc = x.shape[1]
    xc = _causal_dwconv(x, prev, pos, cw_ref, cb_ref, width)
    ga, gx = [], []
    for nb in range(tc // RNN_BLOCK_W):
        xb = xc[:, nb * RNN_BLOCK_W:(nb + 1) * RNN_BLOCK_W].astype(jnp.bfloat16)
        ga.append(jnp.dot(xb, wa_ref[nb], preferred_element_type=jnp.float32))
        gx.append(jnp.dot(xb, wx_ref[nb], preferred_element_type=jnp.float32))
    r_gate = jax.nn.sigmoid(jnp.concatenate(ga, axis=1) + ba_ref[...])
    i_gate = jax.nn.sigmoid(jnp.concatenate(gx, axis=1) + bx_ref[...])
    log_a = -RG_C * r_gate * jax.nn.softplus(-lam_ref[...])
    a = jnp.exp(log_a)
    u = jnp.sqrt(-jnp.tanh(log_a) * (a * a + 1.0)) * (i_gate * xc)
    sub = pos % SUBLANES
    s = 1
    while s < SUBLANES:
        keep = sub >= s
        a_prev = jnp.where(keep, pltpu.roll(a, s, axis=0), 1.0)
        u_prev = jnp.where(keep, pltpu.roll(u, s, axis=0), 0.0)
        u = a * u_prev + u
        a = a * a_prev
        s *= 2
    if period == SUBLANES:
        h = a * h_in + u
    else:
        groups = []
        carry = h_in
        for g in range(period // SUBLANES):
            rows = slice(g * SUBLANES, (g + 1) * SUBLANES)
            blk = a[rows, :] * carry + u[rows, :]
            groups.append(blk)
            carry = blk[SUBLANES - 1:SUBLANES, :]
        h = jnp.concatenate(groups, axis=0)
    return h, h * jax.nn.gelu(xg)


def _rglru_prompt_kernel(xr_ref, xg_ref, cw_ref, cb_ref, wa_ref, wx_ref, ba_ref, bx_ref, lam_ref,
                         ya_ref, hl_ref, prev_ref, hc_ref, *, width):
    tt = xr_ref.shape[0]

    @pl.when(pl.program_id(2) == 0)
    def _():
        prev_ref[...] = jnp.zeros_like(prev_ref)
        hc_ref[...] = jnp.zeros_like(hc_ref)

    x = xr_ref[...]
    pos = lax.broadcasted_iota(jnp.int32, x.shape, 0)
    h_in = hc_ref[SUBLANES - 1:SUBLANES, :]
    h, ya = _rglru_core(x, xg_ref[...], prev_ref[...], pos, tt, h_in, cw_ref, cb_ref, wa_ref, wx_ref,
                        ba_ref, bx_ref, lam_ref, width)
    ya_ref[...] = ya.astype(ya_ref.dtype)
    hc_ref[...] = h[tt - SUBLANES:, :]
    hl_ref[...] = h[tt - 1:tt, :]
    prev_ref[0:SUBLANES, :] = _tail_rows(x, width)


def _rglru_sample_kernel(xr_ref, xg_ref, cw_ref, cb_ref, wa_ref, wx_ref, ba_ref, bx_ref, lam_ref,
                         prev_ref, h0_ref, ya_ref, hl_ref, hs_ref, *, width, seq):
    x = xr_ref[...]
    pos = lax.broadcasted_iota(jnp.int32, x.shape, 0) % seq
    h, ya = _rglru_core(x, xg_ref[...], prev_ref[...], pos, seq, h0_ref[...], cw_ref, cb_ref, wa_ref,
                        wx_ref, ba_ref, bx_ref, lam_ref, width)
    ya_ref[...] = ya.astype(ya_ref.dtype)
    for j in range(h.shape[1] // LANES):
        hs_ref[j] = h[:, j * LANES:(j + 1) * LANES]
        hl_ref[:, j * LANES:(j + 1) * LANES] = hs_ref[j, pl.ds(seq - 1, hl_ref.shape[0], stride=seq), :]


def rglru_prompt(proj, batch, seq, d_rnn, conv_w, conv_b, w_a, w_x, b_a, b_x, lam):
    m = proj.shape[0]
    width = conv_w.shape[0]
    tt = _tile(seq, 512)
    tc = _tile(d_rnn, 256)
    nt, nc = seq // tt, d_rnn // tc
    nb = tc // RNN_BLOCK_W
    row = lambda c, b, t: (b * nt + t, c)
    vec = lambda c, b, t: (0, c)
    ya, hl = pl.pallas_call(
        functools.partial(_rglru_prompt_kernel, width=width),
        grid=(nc, batch, nt),
        in_specs=[pl.BlockSpec((tt, tc), row),
                  pl.BlockSpec((tt, tc), lambda c, b, t: (b * nt + t, nc + c)),
                  pl.BlockSpec((width, tc), vec),
                  pl.BlockSpec((1, tc), vec),
                  pl.BlockSpec((nb, RNN_BLOCK_W, RNN_BLOCK_W), lambda c, b, t: (c, 0, 0)),
                  pl.BlockSpec((nb, RNN_BLOCK_W, RNN_BLOCK_W), lambda c, b, t: (c, 0, 0)),
                  pl.BlockSpec((1, tc), vec),
                  pl.BlockSpec((1, tc), vec),
                  pl.BlockSpec((1, tc), vec)],
        out_specs=[pl.BlockSpec((tt, tc), row),
                   pl.BlockSpec((None, 1, tc), lambda c, b, t: (b, 0, c))],
        out_shape=[jax.ShapeDtypeStruct((m, d_rnn), jnp.bfloat16),
                   jax.ShapeDtypeStruct((batch, 1, d_rnn), jnp.float32)],
        scratch_shapes=[pltpu.VMEM((tt, tc), jnp.float32),
                        pltpu.VMEM((SUBLANES, tc), jnp.float32)],
        compiler_params=_cparams(("arbitrary", "arbitrary", "arbitrary")),
        name="rglru_prompt",
    )(proj, proj, conv_w, conv_b, w_a, w_x, b_a, b_x, lam)
    return ya, hl.reshape(batch, d_rnn)


def rglru_sample(proj, batch, seq, d_rnn, conv_w, conv_b, w_a, w_x, b_a, b_x, lam, prev_rows, h0_rows):
    m = proj.shape[0]
    width = conv_w.shape[0]
    tt = _tile(m, 512)
    tc = _tile(d_rnn, 256)
    nc = d_rnn // tc
    nb = tc // RNN_BLOCK_W
    row = lambda c, t: (t, c)
    vec = lambda c, t: (0, c)
    ya, hl = pl.pallas_call(
        functools.partial(_rglru_sample_kernel, width=width, seq=seq),
        grid=(nc, m // tt),
        in_specs=[pl.BlockSpec((tt, tc), row),
                  pl.BlockSpec((tt, tc), lambda c, t: (t, nc + c)),
                  pl.BlockSpec((width, tc), vec),
                  pl.BlockSpec((1, tc), vec),
                  pl.BlockSpec((nb, RNN_BLOCK_W, RNN_BLOCK_W), lambda c, t: (c, 0, 0)),
                  pl.BlockSpec((nb, RNN_BLOCK_W, RNN_BLOCK_W), lambda c, t: (c, 0, 0)),
                  pl.BlockSpec((1, tc), vec),
                  pl.BlockSpec((1, tc), vec),
                  pl.BlockSpec((1, tc), vec),
                  pl.BlockSpec((tt, tc), row),
                  pl.BlockSpec((tt, tc), row)],
        out_specs=[pl.BlockSpec((tt, tc), row),
                   pl.BlockSpec((tt // seq, tc), row)],
        out_shape=[jax.ShapeDtypeStruct((m, d_rnn), jnp.bfloat16),
                   jax.ShapeDtypeStruct((batch, d_rnn), jnp.float32)],
        scratch_shapes=[pltpu.VMEM((tc // LANES, tt, LANES), jnp.float32)],
        compiler_params=_cparams(("arbitrary", "arbitrary")),
        name="rglru_sample",
    )(proj, proj, conv_w, conv_b, w_a, w_x, b_a, b_x, lam, prev_rows, h0_rows)
    return ya, hl


def _lane_cumsum(x):
    lane = lax.broadcasted_iota(jnp.int32, x.shape, 1)
    s = 1
    while s < x.shape[1]:
        x = x + jnp.where(lane >= s, pltpu.roll(x, s, axis=1), 0.0)
        s *= 2
    return x


def _cumsum_prompt_kernel(lf_ref, c_ref):
    c_ref[...] = _lane_cumsum(lf_ref[...])


def cumsum_prompt(lf_rows):
    b, h, t = lf_rows.shape
    return pl.pallas_call(
        _cumsum_prompt_kernel,
        grid=(b,),
        in_specs=[pl.BlockSpec((None, h, t), lambda i: (i, 0, 0))],
        out_specs=pl.BlockSpec((None, h, t), lambda i: (i, 0, 0)),
        out_shape=jax.ShapeDtypeStruct((b, h, t), jnp.float32),
        compiler_params=_cparams(("arbitrary",)),
        name="cumsum_prompt",
    )(lf_rows)


def _attn_prompt_kernel(q_ref, k_ref, v_ref, cq_ref, ck_ref, o_ref, q2_ref, cqw_ref, m_ref, l_ref, acc_ref,
                        *, n_heads):
    qi, ki = pl.program_id(1), pl.program_id(2)
    tq, tk = q_ref.shape[0], k_ref.shape[0]

    wide = (tq, LANES)

    @pl.when(ki == 0)
    def _():
        q2_ref[...] = (q_ref[...] * (HEAD_DIM ** -0.5 * LOG2E)).astype(jnp.bfloat16)
        cq2 = cq_ref[...] * LOG2E
        for h in range(n_heads):
            cqw_ref[h] = jnp.broadcast_to(cq2[:, h:h + 1], wide)
        m_ref[...] = jnp.full_like(m_ref, MASK_VALUE)
        l_ref[...] = jnp.zeros_like(l_ref)
        acc_ref[...] = jnp.zeros_like(acc_ref)

    def step(diagonal):
        k = k_ref[...].astype(jnp.bfloat16)
        v = v_ref[...].astype(jnp.bfloat16)
        ck2 = ck_ref[...] * LOG2E
        if diagonal:
            causal = (lax.broadcasted_iota(jnp.int32, (tq, tk), 1)
                      <= lax.broadcasted_iota(jnp.int32, (tq, tk), 0))
        for h in range(n_heads):
            hs = slice(h * HEAD_DIM, (h + 1) * HEAD_DIM)
            t = _nt_dot(q2_ref[:, hs], k[:, hs]) - ck2[h:h + 1, :]
            if diagonal:
                t = jnp.where(causal, t, MASK_VALUE)
            cq_h = cqw_ref[h]
            m_prev = m_ref[h]
            m_new = jnp.maximum(m_prev, jnp.broadcast_to(jnp.max(t, axis=-1, keepdims=True), wide) + cq_h)
            alpha = jnp.exp2(m_prev - m_new)
            shift = cq_h - m_new
            p = jnp.exp2(t + jnp.concatenate([shift] * (tk // LANES), axis=1))
            l_ref[h] = alpha * l_ref[h] + jnp.broadcast_to(jnp.sum(p, axis=-1, keepdims=True), wide)
            acc_ref[:, hs] = alpha * acc_ref[:, hs] + jnp.dot(p.astype(jnp.bfloat16), v[:, hs],
                                                              preferred_element_type=jnp.float32)
            m_ref[h] = m_new

    @pl.when(ki < qi)
    def _():
        step(False)

    @pl.when(ki == qi)
    def _():
        step(True)
        for h in range(n_heads):
            hs = slice(h * HEAD_DIM, (h + 1) * HEAD_DIM)
            o_ref[:, hs] = (acc_ref[:, hs] / l_ref[h]).astype(o_ref.dtype)


def attention_prompt(proj, batch, seq, col_q, d_attn, c_cols, c_rows):
    m = proj.shape[0]
    n_heads = d_attn // HEAD_DIM
    tq = _tile(seq, 512)
    nq = seq // tq
    cb = col_q // d_attn
    kv_row = lambda b, qi, ki: b * nq + jnp.minimum(ki, qi)
    return pl.pallas_call(
        functools.partial(_attn_prompt_kernel, n_heads=n_heads),
        grid=(batch, nq, nq),
        in_specs=[pl.BlockSpec((tq, d_attn), lambda b, qi, ki: (b * nq + qi, cb)),
                  pl.BlockSpec((tq, d_attn), lambda b, qi, ki: (kv_row(b, qi, ki), cb + 1)),
                  pl.BlockSpec((tq, d_attn), lambda b, qi, ki: (kv_row(b, qi, ki), cb + 2)),
                  pl.BlockSpec((tq, n_heads), lambda b, qi, ki: (b * nq + qi, 0)),
                  pl.BlockSpec((None, n_heads, tq), lambda b, qi, ki: (b, 0, jnp.minimum(ki, qi)))],
        out_specs=pl.BlockSpec((tq, d_attn), lambda b, qi, ki: (b * nq + qi, 0)),
        out_shape=jax.ShapeDtypeStruct((m, d_attn), jnp.bfloat16),
        scratch_shapes=[pltpu.VMEM((tq, d_attn), jnp.bfloat16),
                        pltpu.VMEM((n_heads, tq, LANES), jnp.float32),
                        pltpu.VMEM((n_heads, tq, LANES), jnp.float32),
                        pltpu.VMEM((n_heads, tq, LANES), jnp.float32),
                        pltpu.VMEM((tq, d_attn), jnp.float32)],
        compiler_params=_cparams(("arbitrary", "arbitrary", "arbitrary")),
        name="attention_prompt",
    )(proj, proj, proj, c_cols, c_rows)


def _attn_sample_kernel(pt_ref, *refs, n_heads, n_pages):
    del pt_ref
    q_ref, kn_ref, vn_ref = refs[0:3]
    k_refs = refs[3:3 + n_pages]
    v_refs = refs[3 + n_pages:3 + 2 * n_pages]
    lf_refs = refs[3 + 2 * n_pages:3 + 3 * n_pages]
    lfn_ref, o_ref = refs[3 + 3 * n_pages:]
    steps = q_ref.shape[0]
    page = k_refs[0].shape[0] // n_heads
    past = n_pages * page
    scale = HEAD_DIM ** -0.5

    q = q_ref[...].astype(jnp.bfloat16)
    c = _lane_cumsum(jnp.concatenate([r[...] for r in lf_refs] + [lfn_ref[...]], axis=1))
    ck = c[:, :past]
    cn = c[:, past:]
    sub = lax.broadcasted_iota(jnp.int32, (steps, LANES), 0)
    lane = lax.broadcasted_iota(jnp.int32, (steps, LANES), 1)
    cq = jnp.concatenate(
        [jnp.sum(jnp.where(sub == lane, cn[h:h + 1, :], 0.0), axis=1, keepdims=True) for h in range(n_heads)],
        axis=0)

    def rows_of(c):
        return jnp.concatenate([jnp.broadcast_to(c[h:h + 1, :], (steps, c.shape[1])) for h in range(n_heads)],
                               axis=0)

    def head_of(page_refs, h):
        return jnp.concatenate([r[pl.ds(h, page, stride=n_heads), :] for r in page_refs],
                               axis=0).astype(jnp.bfloat16)

    pad = jnp.zeros((LANES - steps, HEAD_DIM), jnp.bfloat16)

    def new_of(ref, h):
        return jnp.concatenate([ref[:, h * HEAD_DIM:(h + 1) * HEAD_DIM].astype(jnp.bfloat16), pad], axis=0)

    def scores(k_of):
        return jnp.concatenate(
            [_nt_dot(q[:, h * HEAD_DIM:(h + 1) * HEAD_DIM], k_of(h)) for h in range(n_heads)], axis=0) * scale

    s_past = scores(lambda h: head_of(k_refs, h)) + cq - rows_of(ck)
    causal = jnp.concatenate([lane <= sub] * n_heads, axis=0)
    s_new = jnp.where(causal, scores(lambda h: new_of(kn_ref, h)) + cq - rows_of(cn), MASK_VALUE)
    m_row = jnp.maximum(jnp.max(s_past, axis=-1, keepdims=True), jnp.max(s_new, axis=-1, keepdims=True))
    p_past = jnp.exp(s_past - m_row)
    p_new = jnp.exp(s_new - m_row)
    l_row = jnp.sum(p_past, axis=-1, keepdims=True) + jnp.sum(p_new, axis=-1, keepdims=True)
    for h in range(n_heads):
        rows = slice(h * steps, (h + 1) * steps)
        acc = (jnp.dot(p_past[rows, :].astype(jnp.bfloat16), head_of(v_refs, h), preferred_element_type=jnp.float32)
               + jnp.dot(p_new[rows, :].astype(jnp.bfloat16), new_of(vn_ref, h), preferred_element_type=jnp.float32))
        o_ref[:, h * HEAD_DIM:(h + 1) * HEAD_DIM] = (acc / l_row[rows, :]).astype(o_ref.dtype)


def attention_sample(page_table_flat, n_pages, layer, proj, batch, steps, col_q, d_attn, k_pool, v_pool,
                     lf_pool_rows, lf_new_rows):
    m = proj.shape[0]
    n_heads = d_attn // HEAD_DIM
    page = k_pool.shape[2] // n_heads
    cb = col_q // d_attn

    def kv_spec(i):
        return pl.BlockSpec((None, None, page * n_heads, HEAD_DIM),
                            lambda b, pt: (layer, pt[b * n_pages + i], 0, 0))

    def lf_spec(i):
        return pl.BlockSpec((None, None, n_heads, page), lambda b, pt: (layer, pt[b * n_pages + i], 0, 0))

    in_specs = ([pl.BlockSpec((steps, d_attn), lambda b, pt: (b, cb)),
                 pl.BlockSpec((steps, d_attn), lambda b, pt: (b, cb + 1)),
                 pl.BlockSpec((steps, d_attn), lambda b, pt: (b, cb + 2))]
                + [kv_spec(i) for i in range(n_pages)]
                + [kv_spec(i) for i in range(n_pages)]
                + [lf_spec(i) for i in range(n_pages)]
                + [pl.BlockSpec((None, n_heads, LANES), lambda b, pt: (b, 0, 0))])
    return pl.pallas_call(
        functools.partial(_attn_sample_kernel, n_heads=n_heads, n_pages=n_pages),
        grid_spec=pltpu.PrefetchScalarGridSpec(
            num_scalar_prefetch=1,
            grid=(batch,),
            in_specs=in_specs,
            out_specs=pl.BlockSpec((steps, d_attn), lambda b, pt: (b, 0)),
        ),
        out_shape=jax.ShapeDtypeStruct((m, d_attn), jnp.float32),
        compiler_params=_cparams(("arbitrary",)),
        name="attention_sample",
    )(page_table_flat, proj, proj, proj, *([k_pool] * n_pages), *([v_pool] * n_pages),
      *([lf_pool_rows] * n_pages), lf_new_rows)


def _merge_kernel(ya_ref, yb_ref, wa_ref, wb_ref, ga_ref, gb_ref, z_ref, wa16_ref, wb16_ref):
    @pl.when(pl.program_id(1) == 0)
    def _():
        wa16_ref[...] = wa_ref[...].astype(jnp.bfloat16)
        wb16_ref[...] = wb_ref[...].astype(jnp.bfloat16)

    pa = jnp.dot(ya_ref[...].astype(jnp.bfloat16), wa16_ref[...], preferred_element_type=jnp.float32)
    pb = jnp.dot(yb_ref[...].astype(jnp.bfloat16), wb16_ref[...], preferred_element_type=jnp.float32)
    z = jax.nn.sigmoid(ga_ref[...]) * pa + jax.nn.sigmoid(gb_ref[...]) * pb
    z_ref[...] = z.astype(z_ref.dtype)


def branch_merge(ya, yb, w_branch, layer, proj, col_gates):
    m, da = ya.shape
    db = yb.shape[1]
    d = w_branch.shape[3]
    tm = _tile(m, 1024)
    tn = _tile(d, 512)
    g0 = col_gates // tn
    g1 = (col_gates + d) // tn
    return pl.pallas_call(
        _merge_kernel,
        grid=(d // tn, m // tm),
        in_specs=[pl.BlockSpec((tm, da), lambda j, i: (i, 0)),
                  pl.BlockSpec((tm, db), lambda j, i: (i, 0)),
                  pl.BlockSpec((None, None, da, tn), lambda j, i: (layer, 0, 0, j)),
                  pl.BlockSpec((None, None, db, tn), lambda j, i: (layer, 1, 0, j)),
                  pl.BlockSpec((tm, tn), lambda j, i: (i, g0 + j)),
                  pl.BlockSpec((tm, tn), lambda j, i: (i, g1 + j))],
        out_specs=pl.BlockSpec((tm, tn), lambda j, i: (i, j)),
        out_shape=jax.ShapeDtypeStruct((m, d), jnp.bfloat16),
        scratch_shapes=[pltpu.VMEM((da, tn), jnp.bfloat16),
                        pltpu.VMEM((db, tn), jnp.bfloat16)],
        compiler_params=_cparams(("arbitrary", "arbitrary")),
        name="branch_merge",
    )(ya, yb, w_branch, w_branch, proj, proj)


def _matmul_residual_kernel(a_ref, w_ref, x_ref, o_ref):
    o_ref[...] = x_ref[...] + jnp.dot(a_ref[...], w_ref[...], preferred_element_type=jnp.float32)


def matmul_residual(a, w, layer, x):
    m, k = a.shape
    n = w.shape[2]
    tm = _tile(m, 1024)
    tn = _tile(n, 1024 if k <= 2048 else 512)
    return pl.pallas_call(
        _matmul_residual_kernel,
        grid=(m // tm, n // tn),
        in_specs=[pl.BlockSpec((tm, k), lambda i, j: (i, 0)),
                  pl.BlockSpec((None, k, tn), lambda i, j: (layer, 0, j)),
                  pl.BlockSpec((tm, tn), lambda i, j: (i, j))],
        out_specs=pl.BlockSpec((tm, tn), lambda i, j: (i, j)),
        out_shape=jax.ShapeDtypeStruct((m, n), jnp.float32),
        compiler_params=_cparams(("arbitrary", "arbitrary")),
        name="matmul_residual",
    )(a, w, x)


def _ffn_act(up_g, up_v, prev_g, prev_v, pos, cwg_ref, cbg_ref, cwv_ref, cbv_ref, width):
    gate = _causal_dwconv(up_g, prev_g, pos, cwg_ref, cbg_ref, width)
    val = _causal_dwconv(up_v, prev_v, pos, cwv_ref, cbv_ref, width)
    return jax.nn.gelu(gate) * val


def _ffn_up_prompt_kernel(xn_ref, wg_ref, wv_ref, cwg_ref, cbg_ref, cwv_ref, cbv_ref,
                          h_ref, tg_ref, tv_ref, pg_ref, pv_ref, wg16_ref, wv16_ref, *, width):
    @pl.when((pl.program_id(1) == 0) & (pl.program_id(2) == 0))
    def _():
        wg16_ref[...] = wg_ref[...].astype(jnp.bfloat16)
        wv16_ref[...] = wv_ref[...].astype(jnp.bfloat16)

    @pl.when(pl.program_id(2) == 0)
    def _():
        pg_ref[...] = jnp.zeros_like(pg_ref)
        pv_ref[...] = jnp.zeros_like(pv_ref)

    xn = xn_ref[...]
    up_g = jnp.dot(xn, wg16_ref[...], preferred_element_type=jnp.float32)
    up_v = jnp.dot(xn, wv16_ref[...], preferred_element_type=jnp.float32)
    pos = lax.broadcasted_iota(jnp.int32, up_g.shape, 0)
    act = _ffn_act(up_g, up_v, pg_ref[...], pv_ref[...], pos, cwg_ref, cbg_ref, cwv_ref, cbv_ref, width)
    h_ref[...] = act.astype(h_ref.dtype)
    rows = up_g.shape[0]
    tg_ref[...] = up_g[rows - SUBLANES:, :]
    tv_ref[...] = up_v[rows - SUBLANES:, :]
    pg_ref[0:SUBLANES, :] = _tail_rows(up_g, width)
    pv_ref[0:SUBLANES, :] = _tail_rows(up_v, width)


def _ffn_up_sample_kernel(xn_ref, wg_ref, wv_ref, cwg_ref, cbg_ref, cwv_ref, cbv_ref, pg_ref, pv_ref,
                          h_ref, ug_ref, uv_ref, wg16_ref, wv16_ref, *, width, seq):
    @pl.when(pl.program_id(1) == 0)
    def _():
        wg16_ref[...] = wg_ref[...].astype(jnp.bfloat16)
        wv16_ref[...] = wv_ref[...].astype(jnp.bfloat16)

    xn = xn_ref[...]
    up_g = jnp.dot(xn, wg16_ref[...], preferred_element_type=jnp.float32)
    up_v = jnp.dot(xn, wv16_ref[...], preferred_element_type=jnp.float32)
    pos = lax.broadcasted_iota(jnp.int32, up_g.shape, 0) % seq
    act = _ffn_act(up_g, up_v, pg_ref[...], pv_ref[...], pos, cwg_ref, cbg_ref, cwv_ref, cbv_ref, width)
    h_ref[...] = act.astype(h_ref.dtype)
    ug_ref[...] = up_g
    uv_ref[...] = up_v


def ffn_up_prompt(xn, batch, seq, w_up, layer, conv_w, conv_b):
    m, d = xn.shape
    f = w_up.shape[2] // 2
    width = conv_w.shape[0]
    tt = _tile(seq, 1024)
    tn = _tile(f, 512)
    nt, nf = seq // tt, f // tn
    gcol = lambda n, b, t: (0, n)
    vcol = lambda n, b, t: (0, nf + n)
    h, tg, tv = pl.pallas_call(
        functools.partial(_ffn_up_prompt_kernel, width=width),
        grid=(nf, batch, nt),
        in_specs=[pl.BlockSpec((tt, d), lambda n, b, t: (b * nt + t, 0)),
                  pl.BlockSpec((None, d, tn), lambda n, b, t: (layer, 0, n)),
                  pl.BlockSpec((None, d, tn), lambda n, b, t: (layer, 0, nf + n)),
                  pl.BlockSpec((width, tn), gcol),
                  pl.BlockSpec((1, tn), gcol),
                  pl.BlockSpec((width, tn), vcol),
                  pl.BlockSpec((1, tn), vcol)],
        out_specs=[pl.BlockSpec((tt, tn), lambda n, b, t: (b * nt + t, n)),
                   pl.BlockSpec((None, SUBLANES, tn), lambda n, b, t: (b, 0, n)),
                   pl.BlockSpec((None, SUBLANES, tn), lambda n, b, t: (b, 0, n))],
        out_shape=[jax.ShapeDtypeStruct((m, f), jnp.bfloat16),
                   jax.ShapeDtypeStruct((batch, SUBLANES, f), jnp.float32),
                   jax.ShapeDtypeStruct((batch, SUBLANES, f), jnp.float32)],
        scratch_shapes=[pltpu.VMEM((tt, tn), jnp.float32),
                        pltpu.VMEM((tt, tn), jnp.float32),
                        pltpu.VMEM((d, tn), jnp.bfloat16),
                        pltpu.VMEM((d, tn), jnp.bfloat16)],
        compiler_params=_cparams(("arbitrary", "arbitrary", "arbitrary")),
        name="ffn_up_prompt",
    )(xn, w_up, w_up, conv_w, conv_b, conv_w, conv_b)
    return h, tg, tv


def ffn_up_sample(xn, seq, w_up, layer, conv_w, conv_b, prev_rows):
    m, d = xn.shape
    f = w_up.shape[2] // 2
    width = conv_w.shape[0]
    tt = _tile(m, 512)
    tn = _tile(f, 512)
    nf = f // tn
    gcol = lambda n, t: (0, n)
    vcol = lambda n, t: (0, nf + n)
    return pl.pallas_call(
        functools.partial(_ffn_up_sample_kernel, width=width, seq=seq),
        grid=(nf, m // tt),
        in_specs=[pl.BlockSpec((tt, d), lambda n, t: (t, 0)),
                  pl.BlockSpec((None, d, tn), lambda n, t: (layer, 0, n)),
                  pl.BlockSpec((None, d, tn), lambda n, t: (layer, 0, nf + n)),
                  pl.BlockSpec((width, tn), gcol),
                  pl.BlockSpec((1, tn), gcol),
                  pl.BlockSpec((width, tn), vcol),
                  pl.BlockSpec((1, tn), vcol),
                  pl.BlockSpec((tt, tn), lambda n, t: (t, n)),
                  pl.BlockSpec((tt, tn), lambda n, t: (t, nf + n))],
        out_specs=[pl.BlockSpec((tt, tn), lambda n, t: (t, n)),
                   pl.BlockSpec((tt, tn), lambda n, t: (t, n)),
                   pl.BlockSpec((tt, tn), lambda n, t: (t, n))],
        out_shape=[jax.ShapeDtypeStruct((m, f), jnp.bfloat16),
                   jax.ShapeDtypeStruct((m, f), jnp.float32),
                   jax.ShapeDtypeStruct((m, f), jnp.float32)],
        scratch_shapes=[pltpu.VMEM((d, tn), jnp.bfloat16),
                        pltpu.VMEM((d, tn), jnp.bfloat16)],
        compiler_params=_cparams(("arbitrary", "arbitrary")),
        name="ffn_up_sample",
    )(xn, w_up, w_up, conv_w, conv_b, conv_w, conv_b, prev_rows, prev_rows)


def _state_rows(state, seq):
    b, w, c = state.shape
    return jnp.pad(state, ((0, 0), (0, seq - w), (0, 0))).reshape(b * seq, c)


def kernel(x_prompt, x_sample, cache_k, cache_v, cache_logf, page_table, state_rnn_h, state_rnn_conv, state_ffn_conv, norm_mix_g, w_in, rnn_conv_w, rnn_conv_b, rg_w_a, rg_b_a, rg_w_x, rg_b_x, rg_lambda, fox_b_f, w_branch, w_out, norm_ffn_g, w_up, ffn_conv_w, ffn_conv_b, w_down, norm_final_g):
    bp, seq, d = x_prompt.shape
    bs, steps, _ = x_sample.shape
    depth = w_in.shape[0]
    d_rnn = rg_lambda.shape[1]
    n_heads = fox_b_f.shape[1]
    d_attn = n_heads * HEAD_DIM
    n_pages = page_table.shape[1]
    page = cache_k.shape[2]
    n_pool = cache_k.shape[1]
    f = w_down.shape[1]
    assert steps == SUBLANES and d_rnn == d_attn
    col_q = 2 * d_rnn
    col_f = col_q + 3 * d_attn
    col_gates = col_f + n_heads
    bf16 = jnp.bfloat16

    xp = x_prompt.reshape(bp * seq, d)
    xs = x_sample.reshape(bs * steps, d)
    pt_flat = page_table.reshape(-1)
    k_pool = cache_k.reshape(depth, n_pool, page * n_heads, HEAD_DIM)
    v_pool = cache_v.reshape(depth, n_pool, page * n_heads, HEAD_DIM)
    lf_pool_rows = jnp.swapaxes(cache_logf, 2, 3)

    outs = {name: [] for name in ("kp", "vp", "lfp", "hp", "rcp", "fcp", "ks", "vs", "lfs", "hs", "rcs", "fcs")}
    w_in16 = w_in.astype(bf16)
    w_gates16 = w_in[:, :, col_gates:].astype(bf16)
    w_f16 = jnp.pad(w_in[:, :, col_f:col_gates], ((0, 0), (0, 0), (0, LANES - n_heads))).astype(bf16)
    rg_wa16 = rg_w_a.astype(bf16)
    rg_wx16 = rg_w_x.astype(bf16)
    w_out16 = w_out.astype(bf16)
    w_down16 = w_down.astype(bf16)
    fw = ffn_conv_w.shape[1] - 1
    for l in range(depth):
        b_f = jnp.pad(fox_b_f[l], (0, LANES - n_heads)).reshape(1, LANES)
        vec = lambda a: a.reshape(1, -1)
        rg = (rnn_conv_w[l], vec(rnn_conv_b[l]), rg_wa16[l], rg_wx16[l], vec(rg_b_a[l]), vec(rg_b_x[l]),
              vec(rg_lambda[l]))
        fcb = vec(ffn_conv_b[l])
        main_gates = col_f
        inproj_w = (w_in16, w_gates16, w_f16, b_f, l, col_f)

        proj, logf = in_projection(xp, norm_mix_g[l], *inproj_w)
        ya, h_last = rglru_prompt(proj, bp, seq, d_rnn, *rg)
        lf = logf[:, :n_heads]
        c_rows = cumsum_prompt(jnp.swapaxes(lf.reshape(bp, seq, n_heads), 1, 2))
        c_cols = jnp.swapaxes(c_rows, 1, 2).reshape(bp * seq, n_heads)
        yb = attention_prompt(proj, bp, seq, col_q, d_attn, c_cols, c_rows)
        z = branch_merge(ya, yb, w_branch, l, proj, main_gates)
        xp = matmul_residual(z, w_out16, l, xp)
        hid, tail_g, tail_v = ffn_up_prompt(rmsnorm(xp, norm_ffn_g[l], bf16), bp, seq, w_up, l,
                                            ffn_conv_w[l], fcb)
        xp = matmul_residual(hid, w_down16, l, xp)
        proj3 = proj.reshape(bp, seq, -1)
        outs["kp"].append(proj3[:, :, col_q + d_attn:col_q + 2 * d_attn].reshape(bp, seq, n_heads, HEAD_DIM))
        outs["vp"].append(proj3[:, :, col_q + 2 * d_attn:col_q + 3 * d_attn].reshape(bp, seq, n_heads, HEAD_DIM))
        outs["lfp"].append(lf.reshape(bp, seq, n_heads))
        outs["hp"].append(h_last)
        outs["rcp"].append(proj3[:, seq - (rnn_conv_w.shape[1] - 1):, :d_rnn])
        outs["fcp"].append(jnp.concatenate([tail_g[:, SUBLANES - fw:], tail_v[:, SUBLANES - fw:]], axis=-1))

        proj, logf = in_projection(xs, norm_mix_g[l], *inproj_w)
        ya, h_last = rglru_sample(proj, bs, steps, d_rnn, *rg,
                                  _state_rows(state_rnn_conv[l], steps),
                                  jnp.repeat(state_rnn_h[l], steps, axis=0))
        lf = logf[:, :n_heads]
        lf_new_rows = jnp.pad(jnp.swapaxes(lf.reshape(bs, steps, n_heads), 1, 2),
                              ((0, 0), (0, 0), (0, LANES - steps)))
        yb = attention_sample(pt_flat, n_pages, l, proj, bs, steps, col_q, d_attn, k_pool, v_pool,
                              lf_pool_rows, lf_new_rows)
        z = branch_merge(ya, yb, w_branch, l, proj, main_gates)
        xs = matmul_residual(z, w_out16, l, xs)
        hid, up_g, up_v = ffn_up_sample(rmsnorm(xs, norm_ffn_g[l], bf16), steps, w_up, l, ffn_conv_w[l], fcb,
                                        _state_rows(state_ffn_conv[l], steps))
        xs = matmul_residual(hid, w_down16, l, xs)
        proj3 = proj.reshape(bs, steps, -1)
        outs["ks"].append(proj3[:, :, col_q + d_attn:col_q + 2 * d_attn].reshape(bs, steps, n_heads, HEAD_DIM))
        outs["vs"].append(proj3[:, :, col_q + 2 * d_attn:col_q + 3 * d_attn].reshape(bs, steps, n_heads, HEAD_DIM))
        outs["lfs"].append(lf.reshape(bs, steps, n_heads))
        outs["hs"].append(h_last)
        outs["rcs"].append(proj3[:, steps - (rnn_conv_w.shape[1] - 1):, :d_rnn])
        outs["fcs"].append(jnp.concatenate([up_g.reshape(bs, steps, f)[:, steps - fw:],
                                            up_v.reshape(bs, steps, f)[:, steps - fw:]], axis=-1))

    y_prompt = rmsnorm(xp, norm_final_g, jnp.float32).reshape(bp, seq, d)
    y_sample = rmsnorm(xs, norm_final_g, jnp.float32).reshape(bs, steps, d)
    st = lambda name: jnp.stack(outs[name])
    return (y_prompt, y_sample, st("kp"), st("vp"), st("lfp"), st("hp"), st("rcp"), st("fcp"),
            st("ks"), st("vs"), st("lfs"), st("hs"), st("rcs"), st("fcs"))
```

```python
import functools

import jax
import jax.numpy as jnp
from jax import lax
from jax.experimental import pallas as pl
from jax.experimental.pallas import tpu as pltpu

HEAD_DIM = 128
RNN_BLOCK_W = 128
RG_C = 8.0
NORM_EPS = 1e-6
SUBLANES = 8
LANES = 128
MASK_VALUE = -1e30
LOG2E = 1.4426950408889634
VMEM_LIMIT_BYTES = 56 * 1024 * 1024


def _cparams(semantics):
    return pltpu.CompilerParams(dimension_semantics=semantics, vmem_limit_bytes=VMEM_LIMIT_BYTES)


def _tile(dim, pref):
    t = min(dim, pref)
    assert dim % t == 0, (dim, pref)
    return t


def _nt_dot(a, b):
    return lax.dot_general(a, b, (((1,), (1,)), ((), ())), preferred_element_type=jnp.float32)


def _rmsnorm_rows(x, g):
    inv = lax.rsqrt(jnp.mean(x * x, axis=-1, keepdims=True) + NORM_EPS)
    return x * inv * g


def _rmsnorm_kernel(x_ref, g_ref, o_ref):
    o_ref[...] = _rmsnorm_rows(x_ref[...], g_ref[...]).astype(o_ref.dtype)


def rmsnorm(x, g, out_dtype):
    m, d = x.shape
    tm = _tile(m, 512)
    return pl.pallas_call(
        _rmsnorm_kernel,
        grid=(m // tm,),
        in_specs=[pl.BlockSpec((tm, d), lambda i: (i, 0)),
                  pl.BlockSpec((1, d), lambda i: (0, 0))],
        out_specs=pl.BlockSpec((tm, d), lambda i: (i, 0)),
        out_shape=jax.ShapeDtypeStruct((m, d), out_dtype),
        compiler_params=_cparams(("arbitrary",)),
        name="rmsnorm",
    )(x, g.reshape(1, d))


def _inproj_kernel(x_ref, g_ref, wa_ref, wb_ref, wf_ref, bf_ref, proj_ref, logf_ref, xn_ref, *, n_first):
    n = pl.program_id(1)

    @pl.when(n == 0)
    def _():
        xn = _rmsnorm_rows(x_ref[...], g_ref[...]).astype(jnp.bfloat16)
        xn_ref[...] = xn
        f = jnp.dot(xn, wf_ref[...], preferred_element_type=jnp.float32)
        logf_ref[...] = jax.nn.log_sigmoid(f + bf_ref[...])

    @pl.when(n < n_first)
    def _():
        proj_ref[...] = jnp.dot(xn_ref[...], wa_ref[...], preferred_element_type=jnp.float32)

    @pl.when(n >= n_first)
    def _():
        proj_ref[...] = jnp.dot(xn_ref[...], wb_ref[...], preferred_element_type=jnp.float32)


def in_projection(x, g, w_all, w_gates, w_f, b_f, layer, n_first_cols):
    m, d = x.shape
    n_gates = w_gates.shape[2]
    tm = _tile(m, 1024)
    tn = _tile(n_first_cols, 1024)
    assert n_gates % tn == 0
    n_first = n_first_cols // tn
    n_tiles = n_first + n_gates // tn
    return pl.pallas_call(
        functools.partial(_inproj_kernel, n_first=n_first),
        grid=(m // tm, n_tiles),
        in_specs=[pl.BlockSpec((tm, d), lambda i, j: (i, 0)),
                  pl.BlockSpec((1, d), lambda i, j: (0, 0)),
                  pl.BlockSpec((None, d, tn), lambda i, j: (layer, 0, jnp.minimum(j, n_first - 1))),
                  pl.BlockSpec((None, d, tn), lambda i, j: (layer, 0, jnp.maximum(j - n_first, 0))),
                  pl.BlockSpec((None, d, LANES), lambda i, j: (layer, 0, 0)),
                  pl.BlockSpec((1, LANES), lambda i, j: (0, 0))],
        out_specs=[pl.BlockSpec((tm, tn), lambda i, j: (i, j)),
                   pl.BlockSpec((tm, LANES), lambda i, j: (i, 0))],
        out_shape=[jax.ShapeDtypeStruct((m, n_first_cols + n_gates), jnp.float32),
                   jax.ShapeDtypeStruct((m, LANES), jnp.float32)],
        scratch_shapes=[pltpu.VMEM((tm, d), jnp.bfloat16)],
        compiler_params=_cparams(("arbitrary", "arbitrary")),
        name="in_projection",
    )(x, g.reshape(1, d), w_all, w_gates, w_f, b_f)


def _causal_dwconv(x, prev, pos, cw_ref, cb_ref, width):
    rows = x.shape[0]
    y = cb_ref[...] + cw_ref[width - 1:width, :] * x
    for d in range(1, width):
        up = width - 1 - d
        before = prev if up == 0 else pltpu.roll(prev, rows - up, axis=0)
        shifted = jnp.where(pos >= d, pltpu.roll(x, d, axis=0), before)
        y = y + cw_ref[width - 1 - d:width - d, :] * shifted
    return y


def _tail_rows(x, width):
    rows = x.shape[0]
    return pltpu.roll(x[rows - SUBLANES:, :], width - 1, axis=0)


def _rglru_core(x, xg, prev, pos, period, h_in, cw_ref, cb_ref, wa_ref, wx_ref, ba_ref, bx_ref, lam_ref, width):
    tc = x.shape[1]
    xc = _causal_dwconv(x, prev, pos, cw_ref, cb_ref, width)
    ga, gx = [], []
    for nb in range(tc // RNN_BLOCK_W):
        xb = xc[:, nb * RNN_BLOCK_W:(nb + 1) * RNN_BLOCK_W].astype(jnp.bfloat16)
        ga.append(jnp.dot(xb, wa_ref[nb], preferred_element_type=jnp.float32))
        gx.append(jnp.dot(xb, wx_ref[nb], preferred_element_type=jnp.float32))
    r_gate = jax.nn.sigmoid(jnp.concatenate(ga, axis=1) + ba_ref[...])
    i_gate = jax.nn.sigmoid(jnp.concatenate(gx, axis=1) + bx_ref[...])
    log_a = -RG_C * r_gate * jax.nn.softplus(-lam_ref[...])
    a = jnp.exp(log_a)
    u = jnp.sqrt(-jnp.tanh(log_a) * (a * a + 1.0)) * (i_gate * xc)
    sub = pos % SUBLANES
    s = 1
    while s < SUBLANES:
        keep = sub >= s
        a_prev = jnp.where(keep, pltpu.roll(a, s, axis=0), 1.0)
        u_prev = jnp.where(keep, pltpu.roll(u, s, axis=0), 0.0)
        u = a * u_prev + u
        a = a * a_prev
        s *= 2
    if period == SUBLANES:
        h = a * h_in + u
    else:
        groups = []
        carry = h_in
        for g in range(period // SUBLANES):
            rows = slice(g * SUBLANES, (g + 1) * SUBLANES)
            blk = a[rows, :] * carry + u[rows, :]
            groups.append(blk)
            carry = blk[SUBLANES - 1:SUBLANES, :]
        h = jnp.concatenate(groups, axis=0)
    return h, h * jax.nn.gelu(xg)


def _rglru_prompt_kernel(xr_ref, xg_ref, cw_ref, cb_ref, wa_ref, wx_ref, ba_ref, bx_ref, lam_ref,
                         ya_ref, hl_ref, prev_ref, hc_ref, *, width):
    tt = xr_ref.shape[0]

    @pl.when(pl.program_id(2) == 0)
    def _():
        prev_ref[...] = jnp.zeros_like(prev_ref)
        hc_ref[...] = jnp.zeros_like(hc_ref)

    x = xr_ref[...]
    pos = lax.broadcasted_iota(jnp.int32, x.shape, 0)
    h_in = hc_ref[SUBLANES - 1:SUBLANES, :]
    h, ya = _rglru_core(x, xg_ref[...], prev_ref[...], pos, tt, h_in, cw_ref, cb_ref, wa_ref, wx_ref,
                        ba_ref, bx_ref, lam_ref, width)
    ya_ref[...] = ya.astype(ya_ref.dtype)
    hc_ref[...] = h[tt - SUBLANES:, :]
    hl_ref[...] = h[tt - 1:tt, :]
    prev_ref[0:SUBLANES, :] = _tail_rows(x, width)


def _rglru_sample_kernel(xr_ref, xg_ref, cw_ref, cb_ref, wa_ref, wx_ref, ba_ref, bx_ref, lam_ref,
                         prev_ref, h0_ref, ya_ref, hl_ref, hs_ref, *, width, seq):
    x = xr_ref[...]
    pos = lax.broadcasted_iota(jnp.int32, x.shape, 0) % seq
    h, ya = _rglru_core(x, xg_ref[...], prev_ref[...], pos, seq, h0_ref[...], cw_ref, cb_ref, wa_ref,
                        wx_ref, ba_ref, bx_ref, lam_ref, width)
    ya_ref[...] = ya.astype(ya_ref.dtype)
    for j in range(h.shape[1] // LANES):
        hs_ref[j] = h[:, j * LANES:(j + 1) * LANES]
        hl_ref[:, j * LANES:(j + 1) * LANES] = hs_ref[j, pl.ds(seq - 1, hl_ref.shape[0], stride=seq), :]


def rglru_prompt(proj, batch, seq, d_rnn, conv_w, conv_b, w_a, w_x, b_a, b_x, lam):
    m = proj.shape[0]
    width = conv_w.shape[0]
    tt = _tile(seq, 512)
    tc = _tile(d_rnn, 256)
    nt, nc = seq // tt, d_rnn // tc
    nb = tc // RNN_BLOCK_W
    row = lambda c, b, t: (b * nt + t, c)
    vec = lambda c, b, t: (0, c)
    ya, hl = pl.pallas_call(
        functools.partial(_rglru_prompt_kernel, width=width),
        grid=(nc, batch, nt),
        in_specs=[pl.BlockSpec((tt, tc), row),
                  pl.BlockSpec((tt, tc), lambda c, b, t: (b * nt + t, nc + c)),
                  pl.BlockSpec((width, tc), vec),
                  pl.BlockSpec((1, tc), vec),
                  pl.BlockSpec((nb, RNN_BLOCK_W, RNN_BLOCK_W), lambda c, b, t: (c, 0, 0)),
                  pl.BlockSpec((nb, RNN_BLOCK_W, RNN_BLOCK_W), lambda c, b, t: (c, 0, 0)),
                  pl.BlockSpec((1, tc), vec),
                  pl.BlockSpec((1, tc), vec),
                  pl.BlockSpec((1, tc), vec)],
        out_specs=[pl.BlockSpec((tt, tc), row),
                   pl.BlockSpec((None, 1, tc), lambda c, b, t: (b, 0, c))],
        out_shape=[jax.ShapeDtypeStruct((m, d_rnn), jnp.bfloat16),
                   jax.ShapeDtypeStruct((batch, 1, d_rnn), jnp.float32)],
        scratch_shapes=[pltpu.VMEM((tt, tc), jnp.float32),
                        pltpu.VMEM((SUBLANES, tc), jnp.float32)],
        compiler_params=_cparams(("arbitrary", "arbitrary", "arbitrary")),
        name="rglru_prompt",
    )(proj, proj, conv_w, conv_b, w_a, w_x, b_a, b_x, lam)
    return ya, hl.reshape(batch, d_rnn)


def rglru_sample(proj, batch, seq, d_rnn, conv_w, conv_b, w_a, w_x, b_a, b_x, lam, prev_rows, h0_rows):
    m = proj.shape[0]
    width = conv_w.shape[0]
    tt = _tile(m, 512)
    tc = _tile(d_rnn, 256)
    nc = d_rnn // tc
    nb = tc // RNN_BLOCK_W
    row = lambda c, t: (t, c)
    vec = lambda c, t: (0, c)
    ya, hl = pl.pallas_call(
        functools.partial(_rglru_sample_kernel, width=width, seq=seq),
        grid=(nc, m // tt),
        in_specs=[pl.BlockSpec((tt, tc), row),
                  pl.BlockSpec((tt, tc), lambda c, t: (t, nc + c)),
                  pl.BlockSpec((width, tc), vec),
                  pl.BlockSpec((1, tc), vec),
                  pl.BlockSpec((nb, RNN_BLOCK_W, RNN_BLOCK_W), lambda c, t: (c, 0, 0)),
                  pl.BlockSpec((nb, RNN_BLOCK_W, RNN_BLOCK_W), lambda c, t: (c, 0, 0)),
                  pl.BlockSpec((1, tc), vec),
                  pl.BlockSpec((1, tc), vec),
                  pl.BlockSpec((1, tc), vec),
                  pl.BlockSpec((tt, tc), row),
                  pl.BlockSpec((tt, tc), row)],
        out_specs=[pl.BlockSpec((tt, tc), row),
                   pl.BlockSpec((tt // seq, tc), row)],
        out_shape=[jax.ShapeDtypeStruct((m, d_rnn), jnp.bfloat16),
                   jax.ShapeDtypeStruct((batch, d_rnn), jnp.float32)],
        scratch_shapes=[pltpu.VMEM((tc // LANES, tt, LANES), jnp.float32)],
        compiler_params=_cparams(("arbitrary", "arbitrary")),
        name="rglru_sample",
    )(proj, proj, conv_w, conv_b, w_a, w_x, b_a, b_x, lam, prev_rows, h0_rows)
    return ya, hl


def _lane_cumsum(x):
    lane = lax.broadcasted_iota(jnp.int32, x.shape, 1)
    s = 1
    while s < x.shape[1]:
        x = x + jnp.where(lane >= s, pltpu.roll(x, s, axis=1), 0.0)
        s *= 2
    return x


def _cumsum_prompt_kernel(lf_ref, c_ref):
    c_ref[...] = _lane_cumsum(lf_ref[...])


def cumsum_prompt(lf_rows):
    b, h, t = lf_rows.shape
    return pl.pallas_call(
        _cumsum_prompt_kernel,
        grid=(b,),
        in_specs=[pl.BlockSpec((None, h, t), lambda i: (i, 0, 0))],
        out_specs=pl.BlockSpec((None, h, t), lambda i: (i, 0, 0)),
        out_shape=jax.ShapeDtypeStruct((b, h, t), jnp.float32),
        compiler_params=_cparams(("arbitrary",)),
        name="cumsum_prompt",
    )(lf_rows)


def _attn_prompt_kernel(q_ref, k_ref, v_ref, cq_ref, ck_ref, o_ref, q2_ref, cqw_ref, m_ref, l_ref, acc_ref,
                        *, n_heads):
    qi, ki = pl.program_id(1), pl.program_id(2)
    tq, tk = q_ref.shape[0], k_ref.shape[0]

    wide = (tq, LANES)

    @pl.when(ki == 0)
    def _():
        q2_ref[...] = (q_ref[...] * (HEAD_DIM ** -0.5 * LOG2E)).astype(jnp.bfloat16)
        cq2 = cq_ref[...] * LOG2E
        for h in range(n_heads):
            cqw_ref[h] = jnp.broadcast_to(cq2[:, h:h + 1], wide)
        m_ref[...] = jnp.full_like(m_ref, MASK_VALUE)
        l_ref[...] = jnp.zeros_like(l_ref)
        acc_ref[...] = jnp.zeros_like(acc_ref)

    def step(diagonal):
        k = k_ref[...].astype(jnp.bfloat16)
        v = v_ref[...].astype(jnp.bfloat16)
        ck2 = ck_ref[...] * LOG2E
        if diagonal:
            causal = (lax.broadcasted_iota(jnp.int32, (tq, tk), 1)
                      <= lax.broadcasted_iota(jnp.int32, (tq, tk), 0))
        for h in range(n_heads):
            hs = slice(h * HEAD_DIM, (h + 1) * HEAD_DIM)
            t = _nt_dot(q2_ref[:, hs], k[:, hs]) - ck2[h:h + 1, :]
            if diagonal:
                t = jnp.where(causal, t, MASK_VALUE)
            cq_h = cqw_ref[h]
            m_prev = m_ref[h]
            m_new = jnp.maximum(m_prev, jnp.broadcast_to(jnp.max(t, axis=-1, keepdims=True), wide) + cq_h)
            alpha = jnp.exp2(m_prev - m_new)
            shift = cq_h - m_new
            p = jnp.exp2(t + jnp.concatenate([shift] * (tk // LANES), axis=1))
            l_ref[h] = alpha * l_ref[h] + jnp.broadcast_to(jnp.sum(p, axis=-1, keepdims=True), wide)
            acc_ref[:, hs] = alpha * acc_ref[:, hs] + jnp.dot(p.astype(jnp.bfloat16), v[:, hs],
                                                              preferred_element_type=jnp.float32)
            m_ref[h] = m_new

    @pl.when(ki < qi)
    def _():
        step(False)

    @pl.when(ki == qi)
    def _():
        step(True)
        for h in range(n_heads):
            hs = slice(h * HEAD_DIM, (h + 1) * HEAD_DIM)
            o_ref[:, hs] = (acc_ref[:, hs] / l_ref[h]).astype(o_ref.dtype)


def attention_prompt(proj, batch, seq, col_q, d_attn, c_cols, c_rows):
    m = proj.shape[0]
    n_heads = d_attn // HEAD_DIM
    tq = _tile(seq, 512)
    nq = seq // tq
    cb = col_q // d_attn
    kv_row = lambda b, qi, ki: b * nq + jnp.minimum(ki, qi)
    return pl.pallas_call(
        functools.partial(_attn_prompt_kernel, n_heads=n_heads),
        grid=(batch, nq, nq),
        in_specs=[pl.BlockSpec((tq, d_attn), lambda b, qi, ki: (b * nq + qi, cb)),
                  pl.BlockSpec((tq, d_attn), lambda b, qi, ki: (kv_row(b, qi, ki), cb + 1)),
                  pl.BlockSpec((tq, d_attn), lambda b, qi, ki: (kv_row(b, qi, ki), cb + 2)),
                  pl.BlockSpec((tq, n_heads), lambda b, qi, ki: (b * nq + qi, 0)),
                  pl.BlockSpec((None, n_heads, tq), lambda b, qi, ki: (b, 0, jnp.minimum(ki, qi)))],
        out_specs=pl.BlockSpec((tq, d_attn), lambda b, qi, ki: (b * nq + qi, 0)),
        out_shape=jax.ShapeDtypeStruct((m, d_attn), jnp.bfloat16),
        scratch_shapes=[pltpu.VMEM((tq, d_attn), jnp.bfloat16),
                        pltpu.VMEM((n_heads, tq, LANES), jnp.float32),
                        pltpu.VMEM((n_heads, tq, LANES), jnp.float32),
                        pltpu.VMEM((n_heads, tq, LANES), jnp.float32),
                        pltpu.VMEM((tq, d_attn), jnp.float32)],
        compiler_params=_cparams(("arbitrary", "arbitrary", "arbitrary")),
        name="attention_prompt",
    )(proj, proj, proj, c_cols, c_rows)


def _attn_sample_kernel(pt_ref, *refs, n_heads, n_pages):
    del pt_ref
    q_ref, kn_ref, vn_ref = refs[0:3]
    k_refs = refs[3:3 + n_pages]
    v_refs = refs[3 + n_pages:3 + 2 * n_pages]
    lf_refs = refs[3 + 2 * n_pages:3 + 3 * n_pages]
    lfn_ref, o_ref, lf_scr = refs[3 + 3 * n_pages:]
    steps = q_ref.shape[0]
    n_rows = n_heads * steps
    width = k_refs[0].shape[0]
    step_shift = steps.bit_length() - 1
    assert steps == 1 << step_shift and n_heads & (n_heads - 1) == 0

    q = q_ref[...] * (HEAD_DIM ** -0.5)
    q16 = jnp.concatenate([q[:, h * HEAD_DIM:(h + 1) * HEAD_DIM] for h in range(n_heads)],
                          axis=0).astype(jnp.bfloat16)

    for i, r in enumerate(lf_refs):
        lf_scr[i:i + 1, :] = r[...]
    x = lf_scr[...]
    lane = lax.broadcasted_iota(jnp.int32, x.shape, 1)
    page_row = lax.broadcasted_iota(jnp.int32, x.shape, 0)
    s = n_heads
    while s < width:
        x = x + jnp.where(lane >= s, pltpu.roll(x, s, axis=1), 0.0)
        s *= 2
    tot = jnp.where(lane >= width - n_heads, x, 0.0)
    s = n_heads
    while s < width:
        tot = tot + pltpu.roll(tot, width - s, axis=1)
        s *= 2
    z = tot
    s = 1
    while s < n_pages:
        z = z + jnp.where(page_row >= s, pltpu.roll(z, s, axis=0), 0.0)
        s *= 2
    ck = x + (z - tot)
    total = z[n_pages - 1:n_pages, 0:LANES]

    rel = _lane_cumsum(lfn_ref[...])
    sub = lax.broadcasted_iota(jnp.int32, (steps, LANES), 0)
    lane8 = lax.broadcasted_iota(jnp.int32, (steps, LANES), 1)
    rel_col = jnp.concatenate(
        [jnp.sum(jnp.where(sub == lane8, rel[h:h + 1, :], 0.0), axis=1, keepdims=True) for h in range(n_heads)],
        axis=0)
    row_head = lax.broadcasted_iota(jnp.int32, (n_rows, LANES), 0) >> step_shift
    tot_col = jnp.sum(jnp.where(lax.broadcasted_iota(jnp.int32, (n_rows, LANES), 1) == row_head,
                                jnp.broadcast_to(total, (n_rows, LANES)), 0.0), axis=1, keepdims=True)
    cq = tot_col + rel_col

    own_head = ((lax.broadcasted_iota(jnp.int32, (n_rows, width), 1) & (n_heads - 1))
                == (lax.broadcasted_iota(jnp.int32, (n_rows, width), 0) >> step_shift))
    cq_own = jnp.where(own_head, cq, MASK_VALUE)
    t_pages = []
    m_row = None
    for p in range(n_pages):
        t = _nt_dot(q16, k_refs[p][...].astype(jnp.bfloat16)) + (cq_own - ck[p:p + 1, :])
        t_pages.append(t)
        m_p = jnp.max(t, axis=-1, keepdims=True)
        m_row = m_p if m_row is None else jnp.maximum(m_row, m_p)

    pad = jnp.zeros((LANES - steps, HEAD_DIM), jnp.bfloat16)

    def new_of(ref, h):
        return jnp.concatenate([ref[:, h * HEAD_DIM:(h + 1) * HEAD_DIM].astype(jnp.bfloat16), pad], axis=0)

    rel_rows = jnp.concatenate([jnp.broadcast_to(rel[h:h + 1, :], (steps, LANES)) for h in range(n_heads)], axis=0)
    s_new = jnp.concatenate(
        [_nt_dot(q16[h * steps:(h + 1) * steps, :], new_of(kn_ref, h)) for h in range(n_heads)], axis=0)
    causal = jnp.concatenate([lane8 <= sub] * n_heads, axis=0)
    s_new = jnp.where(causal, s_new + (rel_col - rel_rows), MASK_VALUE)
    m_row = jnp.maximum(m_row, jnp.max(s_new, axis=-1, keepdims=True))

    p_new = jnp.exp(s_new - m_row)
    l_row = jnp.sum(p_new, axis=-1, keepdims=True)
    acc = jnp.concatenate(
        [jnp.dot(p_new[h * steps:(h + 1) * steps, :].astype(jnp.bfloat16), new_of(vn_ref, h),
                 preferred_element_type=jnp.float32) for h in range(n_heads)], axis=0)
    for p in range(n_pages):
        prob = jnp.exp(t_pages[p] - m_row)
        l_row = l_row + jnp.sum(prob, axis=-1, keepdims=True)
        acc = acc + jnp.dot(prob.astype(jnp.bfloat16), v_refs[p][...].astype(jnp.bfloat16),
                            preferred_element_type=jnp.float32)
    out = acc / l_row
    for h in range(n_heads):
        o_ref[:, h * HEAD_DIM:(h + 1) * HEAD_DIM] = out[h * steps:(h + 1) * steps, :].astype(o_ref.dtype)


def attention_sample(page_table_flat, n_pages, layer, proj, batch, steps, col_q, d_attn, k_pool, v_pool,
                     lf_pool_rows, lf_new_rows):
    m = proj.shape[0]
    n_heads = d_attn // HEAD_DIM
    page = k_pool.shape[2] // n_heads
    cb = col_q // d_attn

    def kv_spec(i):
        return pl.BlockSpec((None, None, page * n_heads, HEAD_DIM),
                            lambda b, pt: (layer, pt[b * n_pages + i], 0, 0))

    def lf_spec(i):
        return pl.BlockSpec((None, None, 1, page * n_heads), lambda b, pt: (layer, pt[b * n_pages + i], 0, 0))

    in_specs = ([pl.BlockSpec((steps, d_attn), lambda b, pt: (b, cb)),
                 pl.BlockSpec((steps, d_attn), lambda b, pt: (b, cb + 1)),
                 pl.BlockSpec((steps, d_attn), lambda b, pt: (b, cb + 2))]
                + [kv_spec(i) for i in range(n_pages)]
                + [kv_spec(i) for i in range(n_pages)]
                + [lf_spec(i) for i in range(n_pages)]
                + [pl.BlockSpec((None, n_heads, LANES), lambda b, pt: (b, 0, 0))])
    return pl.pallas_call(
        functools.partial(_attn_sample_kernel, n_heads=n_heads, n_pages=n_pages),
        grid_spec=pltpu.PrefetchScalarGridSpec(
            num_scalar_prefetch=1,
            grid=(batch,),
            in_specs=in_specs,
            out_specs=pl.BlockSpec((steps, d_attn), lambda b, pt: (b, 0)),
            scratch_shapes=[pltpu.VMEM((n_pages, page * n_heads), jnp.float32)],
        ),
        out_shape=jax.ShapeDtypeStruct((m, d_attn), jnp.float32),
        compiler_params=_cparams(("arbitrary",)),
        name="attention_sample",
    )(page_table_flat, proj, proj, proj, *([k_pool] * n_pages), *([v_pool] * n_pages),
      *([lf_pool_rows] * n_pages), lf_new_rows)


def _merge_kernel(ya_ref, yb_ref, wa_ref, wb_ref, ga_ref, gb_ref, z_ref, wa16_ref, wb16_ref):
    @pl.when(pl.program_id(1) == 0)
    def _():
        wa16_ref[...] = wa_ref[...].astype(jnp.bfloat16)
        wb16_ref[...] = wb_ref[...].astype(jnp.bfloat16)

    pa = jnp.dot(ya_ref[...].astype(jnp.bfloat16), wa16_ref[...], preferred_element_type=jnp.float32)
    pb = jnp.dot(yb_ref[...].astype(jnp.bfloat16), wb16_ref[...], preferred_element_type=jnp.float32)
    z = jax.nn.sigmoid(ga_ref[...]) * pa + jax.nn.sigmoid(gb_ref[...]) * pb
    z_ref[...] = z.astype(z_ref.dtype)


def branch_merge(ya, yb, w_branch, layer, proj, col_gates):
    m, da = ya.shape
    db = yb.shape[1]
    d = w_branch.shape[3]
    tm = _tile(m, 1024)
    tn = _tile(d, 512)
    g0 = col_gates // tn
    g1 = (col_gates + d) // tn
    return pl.pallas_call(
        _merge_kernel,
        grid=(d // tn, m // tm),
        in_specs=[pl.BlockSpec((tm, da), lambda j, i: (i, 0)),
                  pl.BlockSpec((tm, db), lambda j, i: (i, 0)),
                  pl.BlockSpec((None, None, da, tn), lambda j, i: (layer, 0, 0, j)),
                  pl.BlockSpec((None, None, db, tn), lambda j, i: (layer, 1, 0, j)),
                  pl.BlockSpec((tm, tn), lambda j, i: (i, g0 + j)),
                  pl.BlockSpec((tm, tn), lambda j, i: (i, g1 + j))],
        out_specs=pl.BlockSpec((tm, tn), lambda j, i: (i, j)),
        out_shape=jax.ShapeDtypeStruct((m, d), jnp.bfloat16),
        scratch_shapes=[pltpu.VMEM((da, tn), jnp.bfloat16),
                        pltpu.VMEM((db, tn), jnp.bfloat16)],
        compiler_params=_cparams(("arbitrary", "arbitrary")),
        name="branch_merge",
    )(ya, yb, w_branch, w_branch, proj, proj)


def _matmul_residual_kernel(a_ref, w_ref, x_ref, o_ref):
    o_ref[...] = x_ref[...] + jnp.dot(a_ref[...], w_ref[...], preferred_element_type=jnp.float32)


def matmul_residual(a, w, layer, x):
    m, k = a.shape
    n = w.shape[2]
    tm = _tile(m, 1024)
    tn = _tile(n, 1024 if k <= 2048 else 512)
    return pl.pallas_call(
        _matmul_residual_kernel,
        grid=(m // tm, n // tn),
        in_specs=[pl.BlockSpec((tm, k), lambda i, j: (i, 0)),
                  pl.BlockSpec((None, k, tn), lambda i, j: (layer, 0, j)),
                  pl.BlockSpec((tm, tn), lambda i, j: (i, j))],
        out_specs=pl.BlockSpec((tm, tn), lambda i, j: (i, j)),
        out_shape=jax.ShapeDtypeStruct((m, n), jnp.float32),
        compiler_params=_cparams(("arbitrary", "arbitrary")),
        name="matmul_residual",
    )(a, w, x)


def _ffn_act(up_g, up_v, prev_g, prev_v, pos, cwg_ref, cbg_ref, cwv_ref, cbv_ref, width):
    gate = _causal_dwconv(up_g, prev_g, pos, cwg_ref, cbg_ref, width)
    val = _causal_dwconv(up_v, prev_v, pos, cwv_ref, cbv_ref, width)
    return jax.nn.gelu(gate) * val


def _ffn_up_prompt_kernel(xn_ref, wg_ref, wv_ref, cwg_ref, cbg_ref, cwv_ref, cbv_ref,
                          h_ref, tg_ref, tv_ref, pg_ref, pv_ref, wg16_ref, wv16_ref, *, width):
    @pl.when((pl.program_id(1) == 0) & (pl.program_id(2) == 0))
    def _():
        wg16_ref[...] = wg_ref[...].astype(jnp.bfloat16)
        wv16_ref[...] = wv_ref[...].astype(jnp.bfloat16)

    @pl.when(pl.program_id(2) == 0)
    def _():
        pg_ref[...] = jnp.zeros_like(pg_ref)
        pv_ref[...] = jnp.zeros_like(pv_ref)

    xn = xn_ref[...]
    up_g = jnp.dot(xn, wg16_ref[...], preferred_element_type=jnp.float32)
    up_v = jnp.dot(xn, wv16_ref[...], preferred_element_type=jnp.float32)
    pos = lax.broadcasted_iota(jnp.int32, up_g.shape, 0)
    act = _ffn_act(up_g, up_v, pg_ref[...], pv_ref[...], pos, cwg_ref, cbg_ref, cwv_ref, cbv_ref, width)
    h_ref[...] = act.astype(h_ref.dtype)
    rows = up_g.shape[0]
    tg_ref[...] = up_g[rows - SUBLANES:, :]
    tv_ref[...] = up_v[rows - SUBLANES:, :]
    pg_ref[0:SUBLANES, :] = _tail_rows(up_g, width)
    pv_ref[0:SUBLANES, :] = _tail_rows(up_v, width)


def _ffn_up_sample_kernel(xn_ref, wg_ref, wv_ref, cwg_ref, cbg_ref, cwv_ref, cbv_ref, pg_ref, pv_ref,
                          h_ref, ug_ref, uv_ref, wg16_ref, wv16_ref, *, width, seq):
    @pl.when(pl.program_id(1) == 0)
    def _():
        wg16_ref[...] = wg_ref[...].astype(jnp.bfloat16)
        wv16_ref[...] = wv_ref[...].astype(jnp.bfloat16)

    xn = xn_ref[...]
    up_g = jnp.dot(xn, wg16_ref[...], preferred_element_type=jnp.float32)
    up_v = jnp.dot(xn, wv16_ref[...], preferred_element_type=jnp.float32)
    pos = lax.broadcasted_iota(jnp.int32, up_g.shape, 0) % seq
    act = _ffn_act(up_g, up_v, pg_ref[...], pv_ref[...], pos, cwg_ref, cbg_ref, cwv_ref, cbv_ref, width)
    h_ref[...] = act.astype(h_ref.dtype)
    ug_ref[...] = up_g
    uv_ref[...] = up_v


def ffn_up_prompt(xn, batch, seq, w_up, layer, conv_w, conv_b):
    m, d = xn.shape
    f = w_up.shape[2] // 2
    width = conv_w.shape[0]
    tt = _tile(seq, 1024)
    tn = _tile(f, 512)
    nt, nf = seq // tt, f // tn
    gcol = lambda n, b, t: (0, n)
    vcol = lambda n, b, t: (0, nf + n)
    h, tg, tv = pl.pallas_call(
        functools.partial(_ffn_up_prompt_kernel, width=width),
        grid=(nf, batch, nt),
        in_specs=[pl.BlockSpec((tt, d), lambda n, b, t: (b * nt + t, 0)),
                  pl.BlockSpec((None, d, tn), lambda n, b, t: (layer, 0, n)),
                  pl.BlockSpec((None, d, tn), lambda n, b, t: (layer, 0, nf + n)),
                  pl.BlockSpec((width, tn), gcol),
                  pl.BlockSpec((1, tn), gcol),
                  pl.BlockSpec((width, tn), vcol),
                  pl.BlockSpec((1, tn), vcol)],
        out_specs=[pl.BlockSpec((tt, tn), lambda n, b, t: (b * nt + t, n)),
                   pl.BlockSpec((None, SUBLANES, tn), lambda n, b, t: (b, 0, n)),
                   pl.BlockSpec((None, SUBLANES, tn), lambda n, b, t: (b, 0, n))],
        out_shape=[jax.ShapeDtypeStruct((m, f), jnp.bfloat16),
                   jax.ShapeDtypeStruct((batch, SUBLANES, f), jnp.float32),
                   jax.ShapeDtypeStruct((batch, SUBLANES, f), jnp.float32)],
        scratch_shapes=[pltpu.VMEM((tt, tn), jnp.float32),
                        pltpu.VMEM((tt, tn), jnp.float32),
                        pltpu.VMEM((d, tn), jnp.bfloat16),
                        pltpu.VMEM((d, tn), jnp.bfloat16)],
        compiler_params=_cparams(("arbitrary", "arbitrary", "arbitrary")),
        name="ffn_up_prompt",
    )(xn, w_up, w_up, conv_w, conv_b, conv_w, conv_b)
    return h, tg, tv


def ffn_up_sample(xn, seq, w_up, layer, conv_w, conv_b, prev_rows):
    m, d = xn.shape
    f = w_up.shape[2] // 2
    width = conv_w.shape[0]
    tt = _tile(m, 512)
    tn = _tile(f, 512)
    nf = f // tn
    gcol = lambda n, t: (0, n)
    vcol = lambda n, t: (0, nf + n)
    return pl.pallas_call(
        functools.partial(_ffn_up_sample_kernel, width=width, seq=seq),
        grid=(nf, m // tt),
        in_specs=[pl.BlockSpec((tt, d), lambda n, t: (t, 0)),
                  pl.BlockSpec((None, d, tn), lambda n, t: (layer, 0, n)),
                  pl.BlockSpec((None, d, tn), lambda n, t: (layer, 0, nf + n)),
                  pl.BlockSpec((width, tn), gcol),
                  pl.BlockSpec((1, tn), gcol),
                  pl.BlockSpec((width, tn), vcol),
                  pl.BlockSpec((1, tn), vcol),
                  pl.BlockSpec((tt, tn), lambda n, t: (t, n)),
                  pl.BlockSpec((tt, tn), lambda n, t: (t, nf + n))],
        out_specs=[pl.BlockSpec((tt, tn), lambda n, t: (t, n)),
                   pl.BlockSpec((tt, tn), lambda n, t: (t, n)),
                   pl.BlockSpec((tt, tn), lambda n, t: (t, n))],
        out_shape=[jax.ShapeDtypeStruct((m, f), jnp.bfloat16),
                   jax.ShapeDtypeStruct((m, f), jnp.float32),
                   jax.ShapeDtypeStruct((m, f), jnp.float32)],
        scratch_shapes=[pltpu.VMEM((d, tn), jnp.bfloat16),
                        pltpu.VMEM((d, tn), jnp.bfloat16)],
        compiler_params=_cparams(("arbitrary", "arbitrary")),
        name="ffn_up_sample",
    )(xn, w_up, w_up, conv_w, conv_b, conv_w, conv_b, prev_rows, prev_rows)


def _state_rows(state, seq):
    b, w, c = state.shape
    return jnp.pad(state, ((0, 0), (0, seq - w), (0, 0))).reshape(b * seq, c)


def kernel(x_prompt, x_sample, cache_k, cache_v, cache_logf, page_table, state_rnn_h, state_rnn_conv, state_ffn_conv, norm_mix_g, w_in, rnn_conv_w, rnn_conv_b, rg_w_a, rg_b_a, rg_w_x, rg_b_x, rg_lambda, fox_b_f, w_branch, w_out, norm_ffn_g, w_up, ffn_conv_w, ffn_conv_b, w_down, norm_final_g):
    bp, seq, d = x_prompt.shape
    bs, steps, _ = x_sample.shape
    depth = w_in.shape[0]
    d_rnn = rg_lambda.shape[1]
    n_heads = fox_b_f.shape[1]
    d_attn = n_heads * HEAD_DIM
    n_pages = page_table.shape[1]
    page = cache_k.shape[2]
    n_pool = cache_k.shape[1]
    f = w_down.shape[1]
    assert steps == SUBLANES and d_rnn == d_attn
    col_q = 2 * d_rnn
    col_f = col_q + 3 * d_attn
    col_gates = col_f + n_heads
    bf16 = jnp.bfloat16

    xp = x_prompt.reshape(bp * seq, d)
    xs = x_sample.reshape(bs * steps, d)
    pt_flat = page_table.reshape(-1)
    k_pool = cache_k.reshape(depth, n_pool, page * n_heads, HEAD_DIM)
    v_pool = cache_v.reshape(depth, n_pool, page * n_heads, HEAD_DIM)
    lf_pool_rows = cache_logf.reshape(depth, n_pool, 1, page * n_heads)

    outs = {name: [] for name in ("kp", "vp", "lfp", "hp", "rcp", "fcp", "ks", "vs", "lfs", "hs", "rcs", "fcs")}
    w_in16 = w_in.astype(bf16)
    w_gates16 = w_in[:, :, col_gates:].astype(bf16)
    w_f16 = jnp.pad(w_in[:, :, col_f:col_gates], ((0, 0), (0, 0), (0, LANES - n_heads))).astype(bf16)
    rg_wa16 = rg_w_a.astype(bf16)
    rg_wx16 = rg_w_x.astype(bf16)
    w_out16 = w_out.astype(bf16)
    w_down16 = w_down.astype(bf16)
    fw = ffn_conv_w.shape[1] - 1
    for l in range(depth):
        b_f = jnp.pad(fox_b_f[l], (0, LANES - n_heads)).reshape(1, LANES)
        vec = lambda a: a.reshape(1, -1)
        rg = (rnn_conv_w[l], vec(rnn_conv_b[l]), rg_wa16[l], rg_wx16[l], vec(rg_b_a[l]), vec(rg_b_x[l]),
              vec(rg_lambda[l]))
        fcb = vec(ffn_conv_b[l])
        main_gates = col_f
        inproj_w = (w_in16, w_gates16, w_f16, b_f, l, col_f)

        proj, logf = in_projection(xp, norm_mix_g[l], *inproj_w)
        ya, h_last = rglru_prompt(proj, bp, seq, d_rnn, *rg)
        lf = logf[:, :n_heads]
        c_rows = cumsum_prompt(jnp.swapaxes(lf.reshape(bp, seq, n_heads), 1, 2))
        c_cols = jnp.swapaxes(c_rows, 1, 2).reshape(bp * seq, n_heads)
        yb = attention_prompt(proj, bp, seq, col_q, d_attn, c_cols, c_rows)
        z = branch_merge(ya, yb, w_branch, l, proj, main_gates)
        xp = matmul_residual(z, w_out16, l, xp)
        hid, tail_g, tail_v = ffn_up_prompt(rmsnorm(xp, norm_ffn_g[l], bf16), bp, seq, w_up, l,
                                            ffn_conv_w[l], fcb)
        xp = matmul_residual(hid, w_down16, l, xp)
        proj3 = proj.reshape(bp, seq, -1)
        outs["kp"].append(proj3[:, :, col_q + d_attn:col_q + 2 * d_attn].reshape(bp, seq, n_heads, HEAD_DIM))
        outs["vp"].append(proj3[:, :, col_q + 2 * d_attn:col_q + 3 * d_attn].reshape(bp, seq, n_heads, HEAD_DIM))
        outs["lfp"].append(lf.reshape(bp, seq, n_heads))
        outs["hp"].append(h_last)
        outs["rcp"].append(proj3[:, seq - (rnn_conv_w.shape[1] - 1):, :d_rnn])
        outs["fcp"].append(jnp.concatenate([tail_g[:, SUBLANES - fw:], tail_v[:, SUBLANES - fw:]], axis=-1))

        proj, logf = in_projection(xs, norm_mix_g[l], *inproj_w)
        ya, h_last = rglru_sample(proj, bs, steps, d_rnn, *rg,
                                  _state_rows(state_rnn_conv[l], steps),
                                  jnp.repeat(state_rnn_h[l], steps, axis=0))
        lf = logf[:, :n_heads]
        lf_new_rows = jnp.pad(jnp.swapaxes(lf.reshape(bs, steps, n_heads), 1, 2),
                              ((0, 0), (0, 0), (0, LANES - steps)))
        yb = attention_sample(pt_flat, n_pages, l, proj, bs, steps, col_q, d_attn, k_pool, v_pool,
                              lf_pool_rows, lf_new_rows)
        z = branch_merge(ya, yb, w_branch, l, proj, main_gates)
        xs = matmul_residual(z, w_out16, l, xs)
        hid, up_g, up_v = ffn_up_sample(rmsnorm(xs, norm_ffn_g[l], bf16), steps, w_up, l, ffn_conv_w[l], fcb,
                                        _state_rows(state_ffn_conv[l], steps))
        xs = matmul_residual(hid, w_down16, l, xs)
        proj3 = proj.reshape(bs, steps, -1)
        outs["ks"].append(proj3[:, :, col_q + d_attn:col_q + 2 * d_attn].reshape(bs, steps, n_heads, HEAD_DIM))
        outs["vs"].append(proj3[:, :, col_q + 2 * d_attn:col_q + 3 * d_attn].reshape(bs, steps, n_heads, HEAD_DIM))
        outs["lfs"].append(lf.reshape(bs, steps, n_heads))
        outs["hs"].append(h_last)
        outs["rcs"].append(proj3[:, steps - (rnn_conv_w.shape[1] - 1):, :d_rnn])
        outs["fcs"].append(jnp.concatenate([up_g.reshape(bs, steps, f)[:, steps - fw:],
                                            up_v.reshape(bs, steps, f)[:, steps - fw:]], axis=-1))

    y_prompt = rmsnorm(xp, norm_final_g, jnp.float32).reshape(bp, seq, d)
    y_sample = rmsnorm(xs, norm_final_g, jnp.float32).reshape(bs, steps, d)
    st = lambda name: jnp.stack(outs[name])
    return (y_prompt, y_sample, st("kp"), st("vp"), st("lfp"), st("hp"), st("rcp"), st("fcp"),
            st("ks"), st("vs"), st("lfs"), st("hs"), st("rcs"), st("fcs"))
```

```python
import functools

import jax
import jax.numpy as jnp
from jax import lax
from jax.experimental import pallas as pl
from jax.experimental.pallas import tpu as pltpu

HEAD_DIM = 128
RNN_BLOCK_W = 128
RG_C = 8.0
NORM_EPS = 1e-6
SUBLANES = 8
LANES = 128
MASK_VALUE = -1e30
LOG2E = 1.4426950408889634
VMEM_LIMIT_BYTES = 56 * 1024 * 1024


def _cparams(semantics):
    return pltpu.CompilerParams(dimension_semantics=semantics, vmem_limit_bytes=VMEM_LIMIT_BYTES)


def _tile(dim, pref):
    t = min(dim, pref)
    assert dim % t == 0, (dim, pref)
    return t


def _nt_dot(a, b):
    return lax.dot_general(a, b, (((1,), (1,)), ((), ())), preferred_element_type=jnp.float32)


def _rmsnorm_rows(x, g):
    inv = lax.rsqrt(jnp.mean(x * x, axis=-1, keepdims=True) + NORM_EPS)
    return x * inv * g


def _rmsnorm_kernel(x_ref, g_ref, o_ref):
    o_ref[...] = _rmsnorm_rows(x_ref[...], g_ref[...]).astype(o_ref.dtype)


def rmsnorm(x, g, out_dtype):
    m, d = x.shape
    tm = _tile(m, 512)
    return pl.pallas_call(
        _rmsnorm_kernel,
        grid=(m // tm,),
        in_specs=[pl.BlockSpec((tm, d), lambda i: (i, 0)),
                  pl.BlockSpec((1, d), lambda i: (0, 0))],
        out_specs=pl.BlockSpec((tm, d), lambda i: (i, 0)),
        out_shape=jax.ShapeDtypeStruct((m, d), out_dtype),
        compiler_params=_cparams(("arbitrary",)),
        name="rmsnorm",
    )(x, g.reshape(1, d))


def _inproj_kernel(x_ref, g_ref, wa_ref, wb_ref, wf_ref, bf_ref, proj_ref, logf_ref, xn_ref, *, n_first):
    n = pl.program_id(1)

    @pl.when(n == 0)
    def _():
        xn = _rmsnorm_rows(x_ref[...], g_ref[...]).astype(jnp.bfloat16)
        xn_ref[...] = xn
        f = jnp.dot(xn, wf_ref[...], preferred_element_type=jnp.float32)
        logf_ref[...] = jax.nn.log_sigmoid(f + bf_ref[...])

    @pl.when(n < n_first)
    def _():
        proj_ref[...] = jnp.dot(xn_ref[...], wa_ref[...], preferred_element_type=jnp.float32)

    @pl.when(n >= n_first)
    def _():
        proj_ref[...] = jnp.dot(xn_ref[...], wb_ref[...], preferred_element_type=jnp.float32)


def in_projection(x, g, w_all, w_gates, w_f, b_f, layer, n_first_cols):
    m, d = x.shape
    n_gates = w_gates.shape[2]
    tm = _tile(m, 1024)
    tn = _tile(n_first_cols, 1024)
    assert n_gates % tn == 0
    n_first = n_first_cols // tn
    n_tiles = n_first + n_gates // tn
    return pl.pallas_call(
        functools.partial(_inproj_kernel, n_first=n_first),
        grid=(m // tm, n_tiles),
        in_specs=[pl.BlockSpec((tm, d), lambda i, j: (i, 0)),
                  pl.BlockSpec((1, d), lambda i, j: (0, 0)),
                  pl.BlockSpec((None, d, tn), lambda i, j: (layer, 0, jnp.minimum(j, n_first - 1))),
                  pl.BlockSpec((None, d, tn), lambda i, j: (layer, 0, jnp.maximum(j - n_first, 0))),
                  pl.BlockSpec((None, d, LANES), lambda i, j: (layer, 0, 0)),
                  pl.BlockSpec((1, LANES), lambda i, j: (0, 0))],
        out_specs=[pl.BlockSpec((tm, tn), lambda i, j: (i, j)),
                   pl.BlockSpec((tm, LANES), lambda i, j: (i, 0))],
        out_shape=[jax.ShapeDtypeStruct((m, n_first_cols + n_gates), jnp.float32),
                   jax.ShapeDtypeStruct((m, LANES), jnp.float32)],
        scratch_shapes=[pltpu.VMEM((tm, d), jnp.bfloat16)],
        compiler_params=_cparams(("arbitrary", "arbitrary")),
        name="in_projection",
    )(x, g.reshape(1, d), w_all, w_gates, w_f, b_f)


def _causal_dwconv(x, prev, pos, cw_ref, cb_ref, width):
    rows = x.shape[0]
    y = cb_ref[...] + cw_ref[width - 1:width, :] * x
    for d in range(1, width):
        up = width - 1 - d
        before = prev if up == 0 else pltpu.roll(prev, rows - up, axis=0)
        shifted = jnp.where(pos >= d, pltpu.roll(x, d, axis=0), before)
        y = y + cw_ref[width - 1 - d:width - d, :] * shifted
    return y


def _tail_rows(x, width):
    rows = x.shape[0]
    return pltpu.roll(x[rows - SUBLANES:, :], width - 1, axis=0)


def _rglru_core(x, xg, prev, pos, period, h_in, cw_ref, cb_ref, wa_ref, wx_ref, ba_ref, bx_ref, lam_ref, width):
    tc = x.shape[1]
    xc = _causal_dwconv(x, prev, pos, cw_ref, cb_ref, width)
    ga, gx = [], []
    for nb in range(tc // RNN_BLOCK_W):
        xb = xc[:, nb * RNN_BLOCK_W:(nb + 1) * RNN_BLOCK_W].astype(jnp.bfloat16)
        ga.append(jnp.dot(xb, wa_ref[nb], preferred_element_type=jnp.float32))
        gx.append(jnp.dot(xb, wx_ref[nb], preferred_element_type=jnp.float32))
    r_gate = jax.nn.sigmoid(jnp.concatenate(ga, axis=1) + ba_ref[...])
    i_gate = jax.nn.sigmoid(jnp.concatenate(gx, axis=1) + bx_ref[...])
    log_a = -RG_C * r_gate * jax.nn.softplus(-lam_ref[...])
    a = jnp.exp(log_a)
    u = jnp.sqrt(-jnp.tanh(log_a) * (a * a + 1.0)) * (i_gate * xc)
    sub = pos % SUBLANES
    s = 1
    while s < SUBLANES:
        keep = sub >= s
        a_prev = jnp.where(keep, pltpu.roll(a, s, axis=0), 1.0)
        u_prev = jnp.where(keep, pltpu.roll(u, s, axis=0), 0.0)
        u = a * u_prev + u
        a = a * a_prev
        s *= 2
    if period == SUBLANES:
        h = a * h_in + u
    else:
        groups = []
        carry = h_in
        for g in range(period // SUBLANES):
            rows = slice(g * SUBLANES, (g + 1) * SUBLANES)
            blk = a[rows, :] * carry + u[rows, :]
            groups.append(blk)
            carry = blk[SUBLANES - 1:SUBLANES, :]
        h = jnp.concatenate(groups, axis=0)
    return h, h * jax.nn.gelu(xg)


def _rglru_prompt_kernel(xr_ref, xg_ref, cw_ref, cb_ref, wa_ref, wx_ref, ba_ref, bx_ref, lam_ref,
                         ya_ref, hl_ref, prev_ref, hc_ref, *, width):
    tt = xr_ref.shape[0]

    @pl.when(pl.program_id(2) == 0)
    def _():
        prev_ref[...] = jnp.zeros_like(prev_ref)
        hc_ref[...] = jnp.zeros_like(hc_ref)

    x = xr_ref[...]
    pos = lax.broadcasted_iota(jnp.int32, x.shape, 0)
    h_in = hc_ref[SUBLANES - 1:SUBLANES, :]
    h, ya = _rglru_core(x, xg_ref[...], prev_ref[...], pos, tt, h_in, cw_ref, cb_ref, wa_ref, wx_ref,
                        ba_ref, bx_ref, lam_ref, width)
    ya_ref[...] = ya.astype(ya_ref.dtype)
    hc_ref[...] = h[tt - SUBLANES:, :]
    hl_ref[...] = h[tt - 1:tt, :]
    prev_ref[0:SUBLANES, :] = _tail_rows(x, width)


def _rglru_sample_kernel(xr_ref, xg_ref, cw_ref, cb_ref, wa_ref, wx_ref, ba_ref, bx_ref, lam_ref,
                         prev_ref, h0_ref, ya_ref, hl_ref, hs_ref, *, width, seq):
    x = xr_ref[...]
    pos = lax.broadcasted_iota(jnp.int32, x.shape, 0) % seq
    h, ya = _rglru_core(x, xg_ref[...], prev_ref[...], pos, seq, h0_ref[...], cw_ref, cb_ref, wa_ref,
                        wx_ref, ba_ref, bx_ref, lam_ref, width)
    ya_ref[...] = ya.astype(ya_ref.dtype)
    for j in range(h.shape[1] // LANES):
        hs_ref[j] = h[:, j * LANES:(j + 1) * LANES]
        hl_ref[:, j * LANES:(j + 1) * LANES] = hs_ref[j, pl.ds(seq - 1, hl_ref.shape[0], stride=seq), :]


def rglru_prompt(proj, batch, seq, d_rnn, conv_w, conv_b, w_a, w_x, b_a, b_x, lam):
    m = proj.shape[0]
    width = conv_w.shape[0]
    tt = _tile(seq, 512)
    tc = _tile(d_rnn, 256)
    nt, nc = seq // tt, d_rnn // tc
    nb = tc // RNN_BLOCK_W
    row = lambda c, b, t: (b * nt + t, c)
    vec = lambda c, b, t: (0, c)
    ya, hl = pl.pallas_call(
        functools.partial(_rglru_prompt_kernel, width=width),
        grid=(nc, batch, nt),
        in_specs=[pl.BlockSpec((tt, tc), row),
                  pl.BlockSpec((tt, tc), lambda c, b, t: (b * nt + t, nc + c)),
                  pl.BlockSpec((width, tc), vec),
                  pl.BlockSpec((1, tc), vec),
                  pl.BlockSpec((nb, RNN_BLOCK_W, RNN_BLOCK_W), lambda c, b, t: (c, 0, 0)),
                  pl.BlockSpec((nb, RNN_BLOCK_W, RNN_BLOCK_W), lambda c, b, t: (c, 0, 0)),
                  pl.BlockSpec((1, tc), vec),
                  pl.BlockSpec((1, tc), vec),
                  pl.BlockSpec((1, tc), vec)],
        out_specs=[pl.BlockSpec((tt, tc), row),
                   pl.BlockSpec((None, 1, tc), lambda c, b, t: (b, 0, c))],
        out_shape=[jax.ShapeDtypeStruct((m, d_rnn), jnp.bfloat16),
                   jax.ShapeDtypeStruct((batch, 1, d_rnn), jnp.float32)],
        scratch_shapes=[pltpu.VMEM((tt, tc), jnp.float32),
                        pltpu.VMEM((SUBLANES, tc), jnp.float32)],
        compiler_params=_cparams(("arbitrary", "arbitrary", "arbitrary")),
        name="rglru_prompt",
    )(proj, proj, conv_w, conv_b, w_a, w_x, b_a, b_x, lam)
    return ya, hl.reshape(batch, d_rnn)


def rglru_sample(proj, batch, seq, d_rnn, conv_w, conv_b, w_a, w_x, b_a, b_x, lam, prev_rows, h0_rows):
    m = proj.shape[0]
    width = conv_w.shape[0]
    tt = _tile(m, 512)
    tc = _tile(d_rnn, 256)
    nc = d_rnn // tc
    nb = tc // RNN_BLOCK_W
    row = lambda c, t: (t, c)
    vec = lambda c, t: (0, c)
    ya, hl = pl.pallas_call(
        functools.partial(_rglru_sample_kernel, width=width, seq=seq),
        grid=(nc, m // tt),
        in_specs=[pl.BlockSpec((tt, tc), row),
                  pl.BlockSpec((tt, tc), lambda c, t: (t, nc + c)),
                  pl.BlockSpec((width, tc), vec),
                  pl.BlockSpec((1, tc), vec),
                  pl.BlockSpec((nb, RNN_BLOCK_W, RNN_BLOCK_W), lambda c, t: (c, 0, 0)),
                  pl.BlockSpec((nb, RNN_BLOCK_W, RNN_BLOCK_W), lambda c, t: (c, 0, 0)),
                  pl.BlockSpec((1, tc), vec),
                  pl.BlockSpec((1, tc), vec),
                  pl.BlockSpec((1, tc), vec),
                  pl.BlockSpec((tt, tc), row),
                  pl.BlockSpec((tt, tc), row)],
        out_specs=[pl.BlockSpec((tt, tc), row),
                   pl.BlockSpec((tt // seq, tc), row)],
        out_shape=[jax.ShapeDtypeStruct((m, d_rnn), jnp.bfloat16),
                   jax.ShapeDtypeStruct((batch, d_rnn), jnp.float32)],
        scratch_shapes=[pltpu.VMEM((tc // LANES, tt, LANES), jnp.float32)],
        compiler_params=_cparams(("arbitrary", "arbitrary")),
        name="rglru_sample",
    )(proj, proj, conv_w, conv_b, w_a, w_x, b_a, b_x, lam, prev_rows, h0_rows)
    return ya, hl


def _lane_cumsum(x):
    lane = lax.broadcasted_iota(jnp.int32, x.shape, 1)
    s = 1
    while s < x.shape[1]:
        x = x + jnp.where(lane >= s, pltpu.roll(x, s, axis=1), 0.0)
        s *= 2
    return x


def _cumsum_prompt_kernel(lf_ref, c_ref):
    c_ref[...] = _lane_cumsum(lf_ref[...])


def cumsum_prompt(lf_rows):
    b, h, t = lf_rows.shape
    return pl.pallas_call(
        _cumsum_prompt_kernel,
        grid=(b,),
        in_specs=[pl.BlockSpec((None, h, t), lambda i: (i, 0, 0))],
        out_specs=pl.BlockSpec((None, h, t), lambda i: (i, 0, 0)),
        out_shape=jax.ShapeDtypeStruct((b, h, t), jnp.float32),
        compiler_params=_cparams(("arbitrary",)),
        name="cumsum_prompt",
    )(lf_rows)


def _attn_prompt_kernel(q_ref, k_ref, v_ref, cq_ref, ck_ref, o_ref, q2_ref, cqw_ref, m_ref, l_ref, acc_ref,
                        *, n_heads):
    qi, ki = pl.program_id(1), pl.program_id(2)
    tq, tk = q_ref.shape[0], k_ref.shape[0]

    wide = (tq, LANES)

    @pl.when(ki == 0)
    def _():
        q2_ref[...] = (q_ref[...] * (HEAD_DIM ** -0.5 * LOG2E)).astype(jnp.bfloat16)
        cq2 = cq_ref[...] * LOG2E
        for h in range(n_heads):
            cqw_ref[h] = jnp.broadcast_to(cq2[:, h:h + 1], wide)
        m_ref[...] = jnp.full_like(m_ref, MASK_VALUE)
        l_ref[...] = jnp.zeros_like(l_ref)
        acc_ref[...] = jnp.zeros_like(acc_ref)

    def step(diagonal):
        k = k_ref[...].astype(jnp.bfloat16)
        v = v_ref[...].astype(jnp.bfloat16)
        ck2 = ck_ref[...] * LOG2E
        if diagonal:
            causal = (lax.broadcasted_iota(jnp.int32, (tq, tk), 1)
                      <= lax.broadcasted_iota(jnp.int32, (tq, tk), 0))
        for h in range(n_heads):
            hs = slice(h * HEAD_DIM, (h + 1) * HEAD_DIM)
            t = _nt_dot(q2_ref[:, hs], k[:, hs]) - ck2[h:h + 1, :]
            if diagonal:
                t = jnp.where(causal, t, MASK_VALUE)
            cq_h = cqw_ref[h]
            m_prev = m_ref[h]
            m_new = jnp.maximum(m_prev, jnp.broadcast_to(jnp.max(t, axis=-1, keepdims=True), wide) + cq_h)
            alpha = jnp.exp2(m_prev - m_new)
            shift = cq_h - m_new
            p = jnp.exp2(t + jnp.concatenate([shift] * (tk // LANES), axis=1))
            l_ref[h] = alpha * l_ref[h] + jnp.broadcast_to(jnp.sum(p, axis=-1, keepdims=True), wide)
            acc_ref[:, hs] = alpha * acc_ref[:, hs] + jnp.dot(p.astype(jnp.bfloat16), v[:, hs],
                                                              preferred_element_type=jnp.float32)
            m_ref[h] = m_new

    @pl.when(ki < qi)
    def _():
        step(False)

    @pl.when(ki == qi)
    def _():
        step(True)
        for h in range(n_heads):
            hs = slice(h * HEAD_DIM, (h + 1) * HEAD_DIM)
            o_ref[:, hs] = (acc_ref[:, hs] / l_ref[h]).astype(o_ref.dtype)


def attention_prompt(proj, batch, seq, col_q, d_attn, c_cols, c_rows):
    m = proj.shape[0]
    n_heads = d_attn // HEAD_DIM
    tq = _tile(seq, 512)
    nq = seq // tq
    cb = col_q // d_attn
    kv_row = lambda b, qi, ki: b * nq + jnp.minimum(ki, qi)
    return pl.pallas_call(
        functools.partial(_attn_prompt_kernel, n_heads=n_heads),
        grid=(batch, nq, nq),
        in_specs=[pl.BlockSpec((tq, d_attn), lambda b, qi, ki: (b * nq + qi, cb)),
                  pl.BlockSpec((tq, d_attn), lambda b, qi, ki: (kv_row(b, qi, ki), cb + 1)),
                  pl.BlockSpec((tq, d_attn), lambda b, qi, ki: (kv_row(b, qi, ki), cb + 2)),
                  pl.BlockSpec((tq, n_heads), lambda b, qi, ki: (b * nq + qi, 0)),
                  pl.BlockSpec((None, n_heads, tq), lambda b, qi, ki: (b, 0, jnp.minimum(ki, qi)))],
        out_specs=pl.BlockSpec((tq, d_attn), lambda b, qi, ki: (b * nq + qi, 0)),
        out_shape=jax.ShapeDtypeStruct((m, d_attn), jnp.bfloat16),
        scratch_shapes=[pltpu.VMEM((tq, d_attn), jnp.bfloat16),
                        pltpu.VMEM((n_heads, tq, LANES), jnp.float32),
                        pltpu.VMEM((n_heads, tq, LANES), jnp.float32),
                        pltpu.VMEM((n_heads, tq, LANES), jnp.float32),
                        pltpu.VMEM((tq, d_attn), jnp.float32)],
        compiler_params=_cparams(("arbitrary", "arbitrary", "arbitrary")),
        name="attention_prompt",
    )(proj, proj, proj, c_cols, c_rows)


def _attn_sample_kernel(pt_ref, *refs, n_heads, n_pages):
    del pt_ref
    q_ref, kn_ref, vn_ref = refs[0:3]
    k_refs = refs[3:3 + n_pages]
    v_refs = refs[3 + n_pages:3 + 2 * n_pages]
    lf_refs = refs[3 + 2 * n_pages:3 + 3 * n_pages]
    lfn_ref, o_ref, lf_scr = refs[3 + 3 * n_pages:]
    steps = q_ref.shape[0]
    n_rows = n_heads * steps
    width = k_refs[0].shape[0]
    step_shift = steps.bit_length() - 1
    assert steps == 1 << step_shift and n_heads & (n_heads - 1) == 0

    q = q_ref[...] * (HEAD_DIM ** -0.5)
    q16 = jnp.concatenate([q[:, h * HEAD_DIM:(h + 1) * HEAD_DIM] for h in range(n_heads)],
                          axis=0).astype(jnp.bfloat16)

    for i, r in enumerate(lf_refs):
        lf_scr[i:i + 1, :] = r[...]
    x = lf_scr[...]
    lane = lax.broadcasted_iota(jnp.int32, x.shape, 1)
    page_row = lax.broadcasted_iota(jnp.int32, x.shape, 0)
    s = n_heads
    while s < width:
        x = x + jnp.where(lane >= s, pltpu.roll(x, s, axis=1), 0.0)
        s *= 2
    tot = jnp.where(lane >= width - n_heads, x, 0.0)
    s = n_heads
    while s < width:
        tot = tot + pltpu.roll(tot, width - s, axis=1)
        s *= 2
    z = tot
    s = 1
    while s < n_pages:
        z = z + jnp.where(page_row >= s, pltpu.roll(z, s, axis=0), 0.0)
        s *= 2
    ck = x + (z - tot)
    total = z[n_pages - 1:n_pages, 0:LANES]

    rel = _lane_cumsum(lfn_ref[...])
    sub = lax.broadcasted_iota(jnp.int32, (steps, LANES), 0)
    lane8 = lax.broadcasted_iota(jnp.int32, (steps, LANES), 1)
    rel_col = jnp.concatenate(
        [jnp.sum(jnp.where(sub == lane8, rel[h:h + 1, :], 0.0), axis=1, keepdims=True) for h in range(n_heads)],
        axis=0)
    row_head = lax.broadcasted_iota(jnp.int32, (n_rows, LANES), 0) >> step_shift
    tot_col = jnp.sum(jnp.where(lax.broadcasted_iota(jnp.int32, (n_rows, LANES), 1) == row_head,
                                jnp.broadcast_to(total, (n_rows, LANES)), 0.0), axis=1, keepdims=True)
    cq = tot_col + rel_col

    own_head = ((lax.broadcasted_iota(jnp.int32, (n_rows, width), 1) & (n_heads - 1))
                == (lax.broadcasted_iota(jnp.int32, (n_rows, width), 0) >> step_shift))
    cq_own = jnp.where(own_head, cq, MASK_VALUE)
    t_pages = []
    m_row = None
    for p in range(n_pages):
        t = _nt_dot(q16, k_refs[p][...].astype(jnp.bfloat16)) + (cq_own - ck[p:p + 1, :])
        t_pages.append(t)
        m_p = jnp.max(t, axis=-1, keepdims=True)
        m_row = m_p if m_row is None else jnp.maximum(m_row, m_p)

    pad = jnp.zeros((LANES - steps, HEAD_DIM), jnp.bfloat16)

    def new_of(ref, h):
        return jnp.concatenate([ref[:, h * HEAD_DIM:(h + 1) * HEAD_DIM].astype(jnp.bfloat16), pad], axis=0)

    rel_rows = jnp.concatenate([jnp.broadcast_to(rel[h:h + 1, :], (steps, LANES)) for h in range(n_heads)], axis=0)
    s_new = jnp.concatenate(
        [_nt_dot(q16[h * steps:(h + 1) * steps, :], new_of(kn_ref, h)) for h in range(n_heads)], axis=0)
    causal = jnp.concatenate([lane8 <= sub] * n_heads, axis=0)
    s_new = jnp.where(causal, s_new + (rel_col - rel_rows), MASK_VALUE)
    m_row = jnp.maximum(m_row, jnp.max(s_new, axis=-1, keepdims=True))

    p_new = jnp.exp(s_new - m_row)
    l_row = jnp.sum(p_new, axis=-1, keepdims=True)
    acc = jnp.concatenate(
        [jnp.dot(p_new[h * steps:(h + 1) * steps, :].astype(jnp.bfloat16), new_of(vn_ref, h),
                 preferred_element_type=jnp.float32) for h in range(n_heads)], axis=0)
    for p in range(n_pages):
        prob = jnp.exp(t_pages[p] - m_row)
        l_row = l_row + jnp.sum(prob, axis=-1, keepdims=True)
        acc = acc + jnp.dot(prob.astype(jnp.bfloat16), v_refs[p][...].astype(jnp.bfloat16),
                            preferred_element_type=jnp.float32)
    out = acc / l_row
    for h in range(n_heads):
        o_ref[:, h * HEAD_DIM:(h + 1) * HEAD_DIM] = out[h * steps:(h + 1) * steps, :].astype(o_ref.dtype)


def attention_sample(page_table_flat, n_pages, layer, proj, batch, steps, col_q, d_attn, k_pool, v_pool,
                     lf_pool_rows, lf_new_rows):
    m = proj.shape[0]
    n_heads = d_attn // HEAD_DIM
    page = k_pool.shape[2] // n_heads
    cb = col_q // d_attn

    def kv_spec(i):
        return pl.BlockSpec((None, None, page * n_heads, HEAD_DIM),
                            lambda b, pt: (layer, pt[b * n_pages + i], 0, 0))

    def lf_spec(i):
        return pl.BlockSpec((None, None, 1, page * n_heads), lambda b, pt: (layer, pt[b * n_pages + i], 0, 0))

    in_specs = ([pl.BlockSpec((steps, d_attn), lambda b, pt: (b, cb)),
                 pl.BlockSpec((steps, d_attn), lambda b, pt: (b, cb + 1)),
                 pl.BlockSpec((steps, d_attn), lambda b, pt: (b, cb + 2))]
                + [kv_spec(i) for i in range(n_pages)]
                + [kv_spec(i) for i in range(n_pages)]
                + [lf_spec(i) for i in range(n_pages)]
                + [pl.BlockSpec((None, n_heads, LANES), lambda b, pt: (b, 0, 0))])
    return pl.pallas_call(
        functools.partial(_attn_sample_kernel, n_heads=n_heads, n_pages=n_pages),
        grid_spec=pltpu.PrefetchScalarGridSpec(
            num_scalar_prefetch=1,
            grid=(batch,),
            in_specs=in_specs,
            out_specs=pl.BlockSpec((steps, d_attn), lambda b, pt: (b, 0)),
            scratch_shapes=[pltpu.VMEM((n_pages, page * n_heads), jnp.float32)],
        ),
        out_shape=jax.ShapeDtypeStruct((m, d_attn), jnp.float32),
        compiler_params=_cparams(("arbitrary",)),
        name="attention_sample",
    )(page_table_flat, proj, proj, proj, *([k_pool] * n_pages), *([v_pool] * n_pages),
      *([lf_pool_rows] * n_pages), lf_new_rows)


def _merge_kernel(ya_ref, yb_ref, wa_ref, wb_ref, ga_ref, gb_ref, z_ref, wa16_ref, wb16_ref):
    @pl.when(pl.program_id(1) == 0)
    def _():
        wa16_ref[...] = wa_ref[...].astype(jnp.bfloat16)
        wb16_ref[...] = wb_ref[...].astype(jnp.bfloat16)

    pa = jnp.dot(ya_ref[...].astype(jnp.bfloat16), wa16_ref[...], preferred_element_type=jnp.float32)
    pb = jnp.dot(yb_ref[...].astype(jnp.bfloat16), wb16_ref[...], preferred_element_type=jnp.float32)
    z = jax.nn.sigmoid(ga_ref[...]) * pa + jax.nn.sigmoid(gb_ref[...]) * pb
    z_ref[...] = z.astype(z_ref.dtype)


def branch_merge(ya, yb, w_branch, layer, proj, col_gates):
    m, da = ya.shape
    db = yb.shape[1]
    d = w_branch.shape[3]
    tm = _tile(m, 1024)
    tn = _tile(d, 512)
    g0 = col_gates // tn
    g1 = (col_gates + d) // tn
    return pl.pallas_call(
        _merge_kernel,
        grid=(d // tn, m // tm),
        in_specs=[pl.BlockSpec((tm, da), lambda j, i: (i, 0)),
                  pl.BlockSpec((tm, db), lambda j, i: (i, 0)),
                  pl.BlockSpec((None, None, da, tn), lambda j, i: (layer, 0, 0, j)),
                  pl.BlockSpec((None, None, db, tn), lambda j, i: (layer, 1, 0, j)),
                  pl.BlockSpec((tm, tn), lambda j, i: (i, g0 + j)),
                  pl.BlockSpec((tm, tn), lambda j, i: (i, g1 + j))],
        out_specs=pl.BlockSpec((tm, tn), lambda j, i: (i, j)),
        out_shape=jax.ShapeDtypeStruct((m, d), jnp.bfloat16),
        scratch_shapes=[pltpu.VMEM((da, tn), jnp.bfloat16),
                        pltpu.VMEM((db, tn), jnp.bfloat16)],
        compiler_params=_cparams(("arbitrary", "arbitrary")),
        name="branch_merge",
    )(ya, yb, w_branch, w_branch, proj, proj)


def _matmul_residual_kernel(a_ref, w_ref, x_ref, o_ref):
    o_ref[...] = x_ref[...] + jnp.dot(a_ref[...], w_ref[...], preferred_element_type=jnp.float32)


def matmul_residual(a, w, layer, x):
    m, k = a.shape
    n = w.shape[2]
    tm = _tile(m, 1024)
    tn = _tile(n, 1024 if k <= 2048 else 512)
    return pl.pallas_call(
        _matmul_residual_kernel,
        grid=(m // tm, n // tn),
        in_specs=[pl.BlockSpec((tm, k), lambda i, j: (i, 0)),
                  pl.BlockSpec((None, k, tn), lambda i, j: (layer, 0, j)),
                  pl.BlockSpec((tm, tn), lambda i, j: (i, j))],
        out_specs=pl.BlockSpec((tm, tn), lambda i, j: (i, j)),
        out_shape=jax.ShapeDtypeStruct((m, n), jnp.float32),
        compiler_params=_cparams(("arbitrary", "arbitrary")),
        name="matmul_residual",
    )(a, w, x)


def _out_proj_norm_kernel(z_ref, w_ref, x_ref, g_ref, o_ref, n_ref):
    y = x_ref[...] + jnp.dot(z_ref[...], w_ref[...], preferred_element_type=jnp.float32)
    o_ref[...] = y
    n_ref[...] = _rmsnorm_rows(y, g_ref[...]).astype(n_ref.dtype)


def out_proj_norm(z, w, layer, x, g):
    m, k = z.shape
    n = w.shape[2]
    tm = _tile(m, 512)
    return pl.pallas_call(
        _out_proj_norm_kernel,
        grid=(m // tm,),
        in_specs=[pl.BlockSpec((tm, k), lambda i: (i, 0)),
                  pl.BlockSpec((None, k, n), lambda i: (layer, 0, 0)),
                  pl.BlockSpec((tm, n), lambda i: (i, 0)),
                  pl.BlockSpec((1, n), lambda i: (0, 0))],
        out_specs=[pl.BlockSpec((tm, n), lambda i: (i, 0)),
                   pl.BlockSpec((tm, n), lambda i: (i, 0))],
        out_shape=[jax.ShapeDtypeStruct((m, n), jnp.float32),
                   jax.ShapeDtypeStruct((m, n), jnp.bfloat16)],
        compiler_params=_cparams(("arbitrary",)),
        name="out_proj_norm",
    )(z, w, x, g.reshape(1, n))


def _ffn_act(up_g, up_v, prev_g, prev_v, pos, cwg_ref, cbg_ref, cwv_ref, cbv_ref, width):
    gate = _causal_dwconv(up_g, prev_g, pos, cwg_ref, cbg_ref, width)
    val = _causal_dwconv(up_v, prev_v, pos, cwv_ref, cbv_ref, width)
    return jax.nn.gelu(gate) * val


def _ffn_up_prompt_kernel(xn_ref, wg_ref, wv_ref, cwg_ref, cbg_ref, cwv_ref, cbv_ref,
                          h_ref, tg_ref, tv_ref, wg16_ref, wv16_ref, pg_ref, pv_ref, *, width):
    @pl.when((pl.program_id(1) == 0) & (pl.program_id(2) == 0))
    def _():
        wg16_ref[...] = wg_ref[...].astype(jnp.bfloat16)
        wv16_ref[...] = wv_ref[...].astype(jnp.bfloat16)

    @pl.when(pl.program_id(2) == 0)
    def _():
        pg_ref[...] = jnp.zeros_like(pg_ref)
        pv_ref[...] = jnp.zeros_like(pv_ref)

    xn = xn_ref[...]
    up_g = jnp.dot(xn, wg16_ref[...], preferred_element_type=jnp.float32)
    up_v = jnp.dot(xn, wv16_ref[...], preferred_element_type=jnp.float32)
    pos = lax.broadcasted_iota(jnp.int32, up_g.shape, 0)
    act = _ffn_act(up_g, up_v, pg_ref[...], pv_ref[...], pos, cwg_ref, cbg_ref, cwv_ref, cbv_ref, width)
    h_ref[...] = act.astype(h_ref.dtype)
    rows = up_g.shape[0]
    tg_ref[...] = up_g[rows - SUBLANES:, :]
    tv_ref[...] = up_v[rows - SUBLANES:, :]
    pg_ref[0:SUBLANES, :] = _tail_rows(up_g, width)
    pv_ref[0:SUBLANES, :] = _tail_rows(up_v, width)


def _ffn_up_sample_kernel(xn_ref, wg_ref, wv_ref, cwg_ref, cbg_ref, cwv_ref, cbv_ref, pg_ref, pv_ref,
                          h_ref, ug_ref, uv_ref, *, width, seq):
    xn = xn_ref[...]
    up_g = jnp.dot(xn, wg_ref[...], preferred_element_type=jnp.float32)
    up_v = jnp.dot(xn, wv_ref[...], preferred_element_type=jnp.float32)
    pos = lax.broadcasted_iota(jnp.int32, up_g.shape, 0) % seq
    act = _ffn_act(up_g, up_v, pg_ref[...], pv_ref[...], pos, cwg_ref, cbg_ref, cwv_ref, cbv_ref, width)
    h_ref[...] = act.astype(h_ref.dtype)
    ug_ref[...] = up_g
    uv_ref[...] = up_v


def ffn_up_prompt(xn, batch, seq, w_up, layer, conv_w, conv_b):
    m, d = xn.shape
    f = w_up.shape[2] // 2
    width = conv_w.shape[0]
    tt = _tile(seq, 1024)
    tn = _tile(f, 512)
    nt, nf = seq // tt, f // tn
    gcol = lambda n, b, t: (0, n)
    vcol = lambda n, b, t: (0, nf + n)
    h, tg, tv, wg16, wv16 = pl.pallas_call(
        functools.partial(_ffn_up_prompt_kernel, width=width),
        grid=(nf, batch, nt),
        in_specs=[pl.BlockSpec((tt, d), lambda n, b, t: (b * nt + t, 0)),
                  pl.BlockSpec((None, d, tn), lambda n, b, t: (layer, 0, n)),
                  pl.BlockSpec((None, d, tn), lambda n, b, t: (layer, 0, nf + n)),
                  pl.BlockSpec((width, tn), gcol),
                  pl.BlockSpec((1, tn), gcol),
                  pl.BlockSpec((width, tn), vcol),
                  pl.BlockSpec((1, tn), vcol)],
        out_specs=[pl.BlockSpec((tt, tn), lambda n, b, t: (b * nt + t, n)),
                   pl.BlockSpec((None, SUBLANES, tn), lambda n, b, t: (b, 0, n)),
                   pl.BlockSpec((None, SUBLANES, tn), lambda n, b, t: (b, 0, n)),
                   pl.BlockSpec((d, tn), gcol),
                   pl.BlockSpec((d, tn), gcol)],
        out_shape=[jax.ShapeDtypeStruct((m, f), jnp.bfloat16),
                   jax.ShapeDtypeStruct((batch, SUBLANES, f), jnp.float32),
                   jax.ShapeDtypeStruct((batch, SUBLANES, f), jnp.float32),
                   jax.ShapeDtypeStruct((d, f), jnp.bfloat16),
                   jax.ShapeDtypeStruct((d, f), jnp.bfloat16)],
        scratch_shapes=[pltpu.VMEM((tt, tn), jnp.float32),
                        pltpu.VMEM((tt, tn), jnp.float32)],
        compiler_params=_cparams(("arbitrary", "arbitrary", "arbitrary")),
        name="ffn_up_prompt",
    )(xn, w_up, w_up, conv_w, conv_b, conv_w, conv_b)
    return h, tg, tv, wg16, wv16


def ffn_up_sample(xn, seq, wg16, wv16, conv_w, conv_b, prev_rows):
    m, d = xn.shape
    f = wg16.shape[1]
    width = conv_w.shape[0]
    tt = _tile(m, 512)
    tn = _tile(f, 512)
    nf = f // tn
    gcol = lambda n, t: (0, n)
    vcol = lambda n, t: (0, nf + n)
    return pl.pallas_call(
        functools.partial(_ffn_up_sample_kernel, width=width, seq=seq),
        grid=(nf, m // tt),
        in_specs=[pl.BlockSpec((tt, d), lambda n, t: (t, 0)),
                  pl.BlockSpec((d, tn), gcol),
                  pl.BlockSpec((d, tn), gcol),
                  pl.BlockSpec((width, tn), gcol),
                  pl.BlockSpec((1, tn), gcol),
                  pl.BlockSpec((width, tn), vcol),
                  pl.BlockSpec((1, tn), vcol),
                  pl.BlockSpec((tt, tn), lambda n, t: (t, n)),
                  pl.BlockSpec((tt, tn), lambda n, t: (t, nf + n))],
        out_specs=[pl.BlockSpec((tt, tn), lambda n, t: (t, n)),
                   pl.BlockSpec((tt, tn), lambda n, t: (t, n)),
                   pl.BlockSpec((tt, tn), lambda n, t: (t, n))],
        out_shape=[jax.ShapeDtypeStruct((m, f), jnp.bfloat16),
                   jax.ShapeDtypeStruct((m, f), jnp.float32),
                   jax.ShapeDtypeStruct((m, f), jnp.float32)],
        compiler_params=_cparams(("arbitrary", "arbitrary")),
        name="ffn_up_sample",
    )(xn, wg16, wv16, conv_w, conv_b, conv_w, conv_b, prev_rows, prev_rows)


def _state_rows(state, seq):
    b, w, c = state.shape
    return jnp.pad(state, ((0, 0), (0, seq - w), (0, 0))).reshape(b * seq, c)


def kernel(x_prompt, x_sample, cache_k, cache_v, cache_logf, page_table, state_rnn_h, state_rnn_conv, state_ffn_conv, norm_mix_g, w_in, rnn_conv_w, rnn_conv_b, rg_w_a, rg_b_a, rg_w_x, rg_b_x, rg_lambda, fox_b_f, w_branch, w_out, norm_ffn_g, w_up, ffn_conv_w, ffn_conv_b, w_down, norm_final_g):
    bp, seq, d = x_prompt.shape
    bs, steps, _ = x_sample.shape
    depth = w_in.shape[0]
    d_rnn = rg_lambda.shape[1]
    n_heads = fox_b_f.shape[1]
    d_attn = n_heads * HEAD_DIM
    n_pages = page_table.shape[1]
    page = cache_k.shape[2]
    n_pool = cache_k.shape[1]
    f = w_down.shape[1]
    assert steps == SUBLANES and d_rnn == d_attn
    col_q = 2 * d_rnn
    col_f = col_q + 3 * d_attn
    col_gates = col_f + n_heads
    bf16 = jnp.bfloat16

    xp = x_prompt.reshape(bp * seq, d)
    xs = x_sample.reshape(bs * steps, d)
    pt_flat = page_table.reshape(-1)
    k_pool = cache_k.reshape(depth, n_pool, page * n_heads, HEAD_DIM)
    v_pool = cache_v.reshape(depth, n_pool, page * n_heads, HEAD_DIM)
    lf_pool_rows = cache_logf.reshape(depth, n_pool, 1, page * n_heads)

    outs = {name: [] for name in ("kp", "vp", "lfp", "hp", "rcp", "fcp", "ks", "vs", "lfs", "hs", "rcs", "fcs")}
    w_in16 = w_in.astype(bf16)
    w_gates16 = w_in[:, :, col_gates:].astype(bf16)
    w_f16 = jnp.pad(w_in[:, :, col_f:col_gates], ((0, 0), (0, 0), (0, LANES - n_heads))).astype(bf16)
    rg_wa16 = rg_w_a.astype(bf16)
    rg_wx16 = rg_w_x.astype(bf16)
    w_out16 = w_out.astype(bf16)
    w_down16 = w_down.astype(bf16)
    fw = ffn_conv_w.shape[1] - 1
    for l in range(depth):
        b_f = jnp.pad(fox_b_f[l], (0, LANES - n_heads)).reshape(1, LANES)
        vec = lambda a: a.reshape(1, -1)
        rg = (rnn_conv_w[l], vec(rnn_conv_b[l]), rg_wa16[l], rg_wx16[l], vec(rg_b_a[l]), vec(rg_b_x[l]),
              vec(rg_lambda[l]))
        fcb = vec(ffn_conv_b[l])
        main_gates = col_f
        inproj_w = (w_in16, w_gates16, w_f16, b_f, l, col_f)

        proj, logf = in_projection(xp, norm_mix_g[l], *inproj_w)
        ya, h_last = rglru_prompt(proj, bp, seq, d_rnn, *rg)
        lf = logf[:, :n_heads]
        c_rows = cumsum_prompt(jnp.swapaxes(lf.reshape(bp, seq, n_heads), 1, 2))
        c_cols = jnp.swapaxes(c_rows, 1, 2).reshape(bp * seq, n_heads)
        yb = attention_prompt(proj, bp, seq, col_q, d_attn, c_cols, c_rows)
        z = branch_merge(ya, yb, w_branch, l, proj, main_gates)
        xp, xn = out_proj_norm(z, w_out16, l, xp, norm_ffn_g[l])
        hid, tail_g, tail_v, wg16, wv16 = ffn_up_prompt(xn, bp, seq, w_up, l, ffn_conv_w[l], fcb)
        xp = matmul_residual(hid, w_down16, l, xp)
        proj3 = proj.reshape(bp, seq, -1)
        outs["kp"].append(proj3[:, :, col_q + d_attn:col_q + 2 * d_attn].reshape(bp, seq, n_heads, HEAD_DIM))
        outs["vp"].append(proj3[:, :, col_q + 2 * d_attn:col_q + 3 * d_attn].reshape(bp, seq, n_heads, HEAD_DIM))
        outs["lfp"].append(lf.reshape(bp, seq, n_heads))
        outs["hp"].append(h_last)
        outs["rcp"].append(proj3[:, seq - (rnn_conv_w.shape[1] - 1):, :d_rnn])
        outs["fcp"].append(jnp.concatenate([tail_g[:, SUBLANES - fw:], tail_v[:, SUBLANES - fw:]], axis=-1))

        proj, logf = in_projection(xs, norm_mix_g[l], *inproj_w)
        ya, h_last = rglru_sample(proj, bs, steps, d_rnn, *rg,
                                  _state_rows(state_rnn_conv[l], steps),
                                  jnp.repeat(state_rnn_h[l], steps, axis=0))
        lf = logf[:, :n_heads]
        lf_new_rows = jnp.pad(jnp.swapaxes(lf.reshape(bs, steps, n_heads), 1, 2),
                              ((0, 0), (0, 0), (0, LANES - steps)))
        yb = attention_sample(pt_flat, n_pages, l, proj, bs, steps, col_q, d_attn, k_pool, v_pool,
                              lf_pool_rows, lf_new_rows)
        z = branch_merge(ya, yb, w_branch, l, proj, main_gates)
        xs, xn = out_proj_norm(z, w_out16, l, xs, norm_ffn_g[l])
        hid, up_g, up_v = ffn_up_sample(xn, steps, wg16, wv16, ffn_conv_w[l], fcb,
                                        _state_rows(state_ffn_conv[l], steps))
        xs = matmul_residual(hid, w_down16, l, xs)
        proj3 = proj.reshape(bs, steps, -1)
        outs["ks"].append(proj3[:, :, col_q + d_attn:col_q + 2 * d_attn].reshape(bs, steps, n_heads, HEAD_DIM))
        outs["vs"].append(proj3[:, :, col_q + 2 * d_attn:col_q + 3 * d_attn].reshape(bs, steps, n_heads, HEAD_DIM))
        outs["lfs"].append(lf.reshape(bs, steps, n_heads))
        outs["hs"].append(h_last)
        outs["rcs"].append(proj3[:, steps - (rnn_conv_w.shape[1] - 1):, :d_rnn])
        outs["fcs"].append(jnp.concatenate([up_g.reshape(bs, steps, f)[:, steps - fw:],
                                            up_v.reshape(bs, steps, f)[:, steps - fw:]], axis=-1))

    y_prompt = rmsnorm(xp, norm_final_g, jnp.float32).reshape(bp, seq, d)
    y_sample = rmsnorm(xs, norm_final_g, jnp.float32).reshape(bs, steps, d)
    st = lambda name: jnp.stack(outs[name])
    return (y_prompt, y_sample, st("kp"), st("vp"), st("lfp"), st("hp"), st("rcp"), st("fcp"),
            st("ks"), st("vs"), st("lfs"), st("hs"), st("rcs"), st("fcs"))
```
